```python
import jax, jax.numpy as jnp
from jax import lax
import numpy as np

D_MODEL = 1024
BATCH = 16
SEQ = 2048
DEPTH = 2

N_MIXERS = 2
HEAD_DIM = 64
N_HEADS = D_MODEL // HEAD_DIM
DILATED_GROUPS = ((128, 1), (512, 4), (2048, 16))
N_GROUPS = len(DILATED_GROUPS)
ROPE_THETA = 500000.0
ROPE_DIM = HEAD_DIM // 4
BLK = 64
CONV_WIDTH = 3
FFN_HIDDEN = -(-8 * D_MODEL // (3 * 256)) * 256
NORM_EPS = 1e-6
NEG_INF = -1e30

kernel_name = "hybrid_dilated_attn_shortconv_adaln_encoder"


def rms_norm(x, g):
    xf = x.astype(jnp.float32)
    r = lax.rsqrt(jnp.mean(xf * xf, axis=-1, keepdims=True) + NORM_EPS)
    return (xf * r).astype(x.dtype) * g


def modulate(h, shift, scale):
    return h * (1 + scale) + shift


def rope_partial(x, pos):
    half = ROPE_DIM // 2
    inv = ROPE_THETA ** (-jnp.arange(half, dtype=jnp.float32) * (2.0 / ROPE_DIM))
    ang = pos.astype(jnp.float32)[:, None] * inv[None, :]
    cos = jnp.cos(ang)[None, :, None, :]
    sin = jnp.sin(ang)[None, :, None, :]
    xr = x[..., :ROPE_DIM].astype(jnp.float32)
    x1, x2 = xr[..., :half], xr[..., half:]
    rot = jnp.concatenate([x1 * cos - x2 * sin, x2 * cos + x1 * sin], axis=-1).astype(x.dtype)
    return jnp.concatenate([rot, x[..., ROPE_DIM:]], axis=-1)


def banded_window_attention(q, k, v, half_window):
    N, L, H, Dh = q.shape
    nb = -(-L // BLK)
    Lp = nb * BLK
    qb = jnp.pad(q, ((0, 0), (0, Lp - L), (0, 0), (0, 0))).reshape(N, nb, BLK, H, Dh)

    def neighbourhood(t):
        t = jnp.pad(t, ((0, 0), (BLK, Lp - L + BLK), (0, 0), (0, 0))).reshape(N, nb + 2, BLK, H, Dh)
        return jnp.concatenate([t[:, :-2], t[:, 1:-1], t[:, 2:]], axis=2)

    kb, vb = neighbourhood(k), neighbourhood(v)
    qpos = jnp.arange(nb)[:, None] * BLK + jnp.arange(BLK)[None, :]
    kpos = jnp.arange(nb)[:, None] * BLK - BLK + jnp.arange(3 * BLK)[None, :]
    kp = kpos[:, None, :]
    mask = (jnp.abs(qpos[:, :, None] - kp) <= half_window) & (kp >= 0) & (kp < L)
    s = jnp.einsum('nbqhd,nbkhd->nbhqk', qb, kb, preferred_element_type=jnp.float32) * (Dh ** -0.5)
    s = jnp.where(mask[None, :, None], s, NEG_INF)
    m = jnp.max(s, axis=-1, keepdims=True)
    p = jnp.exp(s - m)
    l = jnp.sum(p, axis=-1, keepdims=True)
    o = jnp.einsum('nbhqk,nbkhd->nbqhd', p.astype(v.dtype), vb, preferred_element_type=jnp.float32)
    o = o / jnp.transpose(l, (0, 1, 3, 2, 4))
    lse = jnp.transpose((m + jnp.log(l))[..., 0], (0, 1, 3, 2))
    o = o.reshape(N, Lp, H, Dh)[:, :L]
    lse = lse.reshape(N, Lp, H)[:, :L]
    return o, lse


def dilated_attention(q, k, v, dilation, half_window):
    B, S, H, Dh = q.shape
    L = S // dilation

    def split(t):
        return t.reshape(B, L, dilation, H, Dh).transpose(0, 2, 1, 3, 4).reshape(B * dilation, L, H, Dh)

    o, lse = banded_window_attention(split(q), split(k), split(v), half_window)
    o = o.reshape(B, dilation, L, H, Dh).transpose(0, 2, 1, 3, 4).reshape(B, S, H, Dh)
    lse = lse.reshape(B, dilation, L, H).transpose(0, 2, 1, 3).reshape(B, S, H)
    return o, lse


def dilated_mixture_mixer(h, w_qkv, w_o, pos):
    B, S, _ = h.shape
    qkv = (h @ w_qkv).reshape(B, S, N_GROUPS, 3, N_HEADS, HEAD_DIM)
    outs, lses = [], []
    for g, (window, dil) in enumerate(DILATED_GROUPS):
        q = rope_partial(qkv[:, :, g, 0], pos)
        k = rope_partial(qkv[:, :, g, 1], pos)
        o, lse = dilated_attention(q, k, qkv[:, :, g, 2], dil, window // (2 * dil))
        outs.append(o)
        lses.append(lse)
    wts = jax.nn.softmax(jnp.stack(lses, axis=0), axis=0)
    o = jnp.einsum('gbsh,gbshd->bshd', wts, jnp.stack(outs, axis=0))
    return o.reshape(B, S, D_MODEL).astype(h.dtype) @ w_o


def short_conv_mixer(h, w_in, conv_w, w_out):
    b_gate, c_gate, u = jnp.split(h @ w_in, 3, axis=-1)
    z = lax.conv_general_dilated(
        c_gate * u, conv_w[:, None, :], window_strides=(1,),
        padding=[((CONV_WIDTH - 1) // 2, (CONV_WIDTH - 1) // 2)],
        dimension_numbers=('NWC', 'WIO', 'NWC'), feature_group_count=D_MODEL)
    return (b_gate * z) @ w_out


def swiglu_ffn(h, w_in, w_out):
    g, u = jnp.split(h @ w_in, 2, axis=-1)
    return (jax.nn.silu(g) * u) @ w_out


def setup_inputs(seed: int = 0) -> dict:
    key = jax.random.key(seed)
    ks = jax.random.split(key, 16)
    D, F = D_MODEL, FFN_HIDDEN
    n_a = (DEPTH + N_MIXERS - 1) // N_MIXERS
    n_b = DEPTH // N_MIXERS
    nrm = lambda k, shape, fan: jax.random.normal(k, shape, jnp.float32) * (fan ** -0.5)
    return {
        "x": jax.random.normal(ks[0], (BATCH, SEQ, D), jnp.float32),
        "c": jax.random.normal(ks[1], (BATCH, D), jnp.float32),
        "attn_w_qkv": nrm(ks[2], (n_a, D, N_GROUPS * 3 * D), D),
        "attn_w_o": nrm(ks[3], (n_a, D, D), D),
        "conv_w_in": nrm(ks[4], (n_b, D, 3 * D), D),
        "conv_w": nrm(ks[5], (n_b, CONV_WIDTH, D), CONV_WIDTH),
        "conv_w_out": nrm(ks[6], (n_b, D, D), D),
        "ada_w": nrm(ks[7], (DEPTH, D, 6 * D), D),
        "ada_b": 0.02 * jax.random.normal(ks[8], (DEPTH, 6 * D), jnp.float32),
        "norm_mix_g": 1.0 + 0.02 * jax.random.normal(ks[9], (DEPTH, D), jnp.float32),
        "norm_ffn_g": 1.0 + 0.02 * jax.random.normal(ks[10], (DEPTH, D), jnp.float32),
        "ffn_w_in": nrm(ks[11], (DEPTH, D, 2 * F), D),
        "ffn_w_out": nrm(ks[12], (DEPTH, F, D), F),
        "final_g": 1.0 + 0.02 * jax.random.normal(ks[13], (D,), jnp.float32),
    }


def reference(x, c, attn_w_qkv, attn_w_o, conv_w_in, conv_w, conv_w_out, ada_w, ada_b,
              norm_mix_g, norm_ffn_g, ffn_w_in, ffn_w_out, final_g):
    pos = jnp.arange(x.shape[1], dtype=jnp.int32)
    cond = jax.nn.silu(c)
    for i in range(DEPTH):
        mod = (cond @ ada_w[i] + ada_b[i])[:, None, :]
        sh1, sc1, g1, sh2, sc2, g2 = jnp.split(mod, 6, axis=-1)
        h = modulate(rms_norm(x, norm_mix_g[i]), sh1, sc1)
        j = i // N_MIXERS
        if i % N_MIXERS == 0:
            y = dilated_mixture_mixer(h, attn_w_qkv[j], attn_w_o[j], pos)
        else:
            y = short_conv_mixer(h, conv_w_in[j], conv_w[j], conv_w_out[j])
        x = x + g1 * y
        h = modulate(rms_norm(x, norm_ffn_g[i]), sh2, sc2)
        x = x + g2 * swiglu_ffn(h, ffn_w_in[i], ffn_w_out[i])
    return rms_norm(x, final_g)
```

```python
import functools

import numpy as np
import jax
import jax.numpy as jnp
from jax import lax
from jax.experimental import pallas as pl
from jax.experimental.pallas import tpu as pltpu

D_MODEL = 1024
HEAD_DIM = 64
N_HEADS = D_MODEL // HEAD_DIM
DILATED_GROUPS = ((128, 1), (512, 4), (2048, 16))
ROPE_THETA = 500000.0
ROPE_DIM = HEAD_DIM // 4
CONV_WIDTH = 3
FFN_HIDDEN = -(-8 * D_MODEL // (3 * 256)) * 256
NORM_EPS = 1e-6
NEG_INF = -1e30

LANES = 128
MXU_WIDTH = 256
HALF_WINDOW = 64
Q_BLOCK = 128
VMEM_LIMIT_BYTES = 56 * 1024 * 1024

assert all(w // (2 * d) == HALF_WINDOW for w, d in DILATED_GROUPS)

_F32 = jnp.float32
_BF16 = jnp.bfloat16


def _compiler_params(n_grid_dims):
    return pltpu.CompilerParams(
        dimension_semantics=("arbitrary",) * n_grid_dims,
        vmem_limit_bytes=VMEM_LIMIT_BYTES,
    )


def _resident(block_shape, index_map):
    return pl.BlockSpec(block_shape, index_map, pipeline_mode=pl.Buffered(1))


def _rms_norm(x, g):
    r = lax.rsqrt(jnp.mean(x * x, axis=-1, keepdims=True) + NORM_EPS)
    return (x * r) * g


def _modulated_norm(x, g, shift, scale):
    return _rms_norm(x, g) * (1.0 + scale) + shift


def _modulation_kernel(c_ref, w_ref, b_ref, o_ref):
    c = c_ref[...]
    cond = (c * jax.nn.sigmoid(c)).astype(_BF16)
    o_ref[...] = jnp.dot(cond, w_ref[...].astype(_BF16), preferred_element_type=_F32) + b_ref[...]


def _modulation(c, ada_w, ada_b):
    depth, d, n = ada_w.shape
    b = c.shape[0]
    tn = 1536
    return pl.pallas_call(
        _modulation_kernel,
        grid=(depth, n // tn),
        in_specs=[
            pl.BlockSpec((b, d), lambda i, j: (0, 0)),
            pl.BlockSpec((None, d, tn), lambda i, j: (i, 0, j)),
            pl.BlockSpec((None, 1, tn), lambda i, j: (i, 0, j)),
        ],
        out_specs=pl.BlockSpec((None, b, tn), lambda i, j: (i, 0, j)),
        out_shape=jax.ShapeDtypeStruct((depth, b, n), _F32),
        compiler_params=_compiler_params(2),
        name="modulation",
    )(c, ada_w, ada_b.reshape(depth, 1, n))


def _rope_tables(seq, dilation, q_scale):
    half = ROPE_DIM // 2
    length = seq // dilation
    inv = ROPE_THETA ** (-np.arange(half, dtype=np.float64) * (2.0 / ROPE_DIM))
    pos = (np.arange(length)[None, :] * dilation + np.arange(dilation)[:, None]).astype(np.float64)
    ang = pos[:, :, None] * inv[None, None, :]
    cos, sin = np.cos(ang), np.sin(ang)
    c = np.ones((dilation, length, HEAD_DIM))
    sa = np.zeros((dilation, length, HEAD_DIM))
    sb = np.zeros((dilation, length, HEAD_DIM))
    c[:, :, :half] = cos
    c[:, :, half:ROPE_DIM] = cos
    sa[:, :, :half] = -sin
    sb[:, :, half:ROPE_DIM] = sin
    k_tab = np.stack([np.tile(t, (1, 1, LANES // HEAD_DIM)) for t in (c, sa, sb)])
    return np.stack([k_tab * q_scale, k_tab]).astype(np.float32)


def _qkv_kernel(x_ref, mod_ref, g_ref, w_ref, tab_ref, o_ref, hs_ref, *, dilation, tm):
    d = D_MODEL
    x = x_ref[0]
    h = _modulated_norm(x, g_ref[...], mod_ref[:, 0:d], mod_ref[:, d:2 * d])
    rows = tm // dilation
    n_slabs = d // LANES
    if dilation == 1:
        hb = h.astype(_BF16)
    else:
        for c in range(n_slabs):
            hs_ref[c] = h[:, c * LANES:(c + 1) * LANES]
        hb = jnp.concatenate(
            [jnp.concatenate([hs_ref[c, pl.ds(r, rows, stride=dilation), :] for c in range(n_slabs)], axis=1)
             for r in range(dilation)], axis=0).astype(_BF16)
    for part in range(3):
        for j in range(d // MXU_WIDTH):
            col = part * d + j * MXU_WIDTH
            res = jnp.dot(hb, w_ref[:, col:col + MXU_WIDTH], preferred_element_type=_F32)
            if part < 2:
                halves = []
                for s in range(MXU_WIDTH // LANES):
                    xs = res[:, s * LANES:(s + 1) * LANES]
                    tabs = [jnp.concatenate([tab_ref[part, t, r] for r in range(dilation)], axis=0)
                            for t in range(3)]
                    halves.append(xs * tabs[0]
                                  + pltpu.roll(xs, LANES - ROPE_DIM // 2, 1) * tabs[1]
                                  + pltpu.roll(xs, ROPE_DIM // 2, 1) * tabs[2])
                res = jnp.concatenate(halves, axis=1)
            res = res.astype(_BF16)
            for r in range(dilation):
                o_ref[0, r, :, col:col + MXU_WIDTH] = res[r * rows:(r + 1) * rows]


def _qkv_group(x, mod, g, w, dilation):
    b, s, d = x.shape
    tm = 512
    length = s // dilation
    tabs = jnp.asarray(_rope_tables(s, dilation, HEAD_DIM ** -0.5))
    kernel = functools.partial(_qkv_kernel, dilation=dilation, tm=tm)
    return pl.pallas_call(
        kernel,
        grid=(b, s // tm),
        in_specs=[
            pl.BlockSpec((1, tm, d), lambda i, j: (i, j, 0)),
            pl.BlockSpec((None, 1, 6 * d), lambda i, j: (i, 0, 0)),
            pl.BlockSpec((1, d), lambda i, j: (0, 0)),
            _resident((d, 3 * d), lambda i, j: (0, 0)),
            pl.BlockSpec((2, 3, dilation, tm // dilation, LANES), lambda i, j: (0, 0, 0, j, 0)),
        ],
        out_specs=pl.BlockSpec((1, dilation, tm // dilation, 3 * d), lambda i, j: (i, 0, j, 0)),
        out_shape=jax.ShapeDtypeStruct((b, dilation, length, 3 * d), _BF16),
        scratch_shapes=[pltpu.VMEM((d // LANES, tm, LANES), _F32)],
        compiler_params=_compiler_params(2),
        name=f"qkv_dil{dilation}",
    )(x, mod, g, w, tabs)


def _band_bias(n_keys):
    i = np.arange(Q_BLOCK)[:, None]
    j = np.arange(n_keys)[None, :]
    return np.stack([np.where(np.abs(off + i - j) <= HALF_WINDOW, 0.0, NEG_INF)
                     for off in (0, HALF_WINDOW, 2 * HALF_WINDOW)]).astype(np.float32)


def _attention_kernel(q0, k0, v0, q1, k1, v1, q2, k2, v2, bias_w_ref, bias_n_ref, o_ref,
                      m_ref, l_ref, acc_ref, *, seq):
    lane = lax.broadcasted_iota(jnp.int32, (Q_BLOCK, LANES), 1)
    in_head = [lane < HEAD_DIM, lane >= HEAD_DIM]

    def run_group(q_ref, k_ref, v_ref, dilation, first, final):
        length = seq // dilation
        n_blocks = length // Q_BLOCK
        n_keys = min(length, Q_BLOCK + 2 * HALF_WINDOW)

        def body(idx, carry):
            r = idx // n_blocks
            j = idx % n_blocks
            m0 = pl.multiple_of(j * Q_BLOCK, Q_BLOCK)
            k_start = pl.multiple_of(jnp.clip(m0 - HALF_WINDOW, 0, length - n_keys), HALF_WINDOW)
            if n_keys == length:
                bias = bias_n_ref[0]
            else:
                bias = bias_w_ref[(m0 - k_start) // HALF_WINDOW]
            q = q_ref[0, r, pl.ds(m0, Q_BLOCK), :]
            k = k_ref[0, r, pl.ds(k_start, n_keys), :]
            v = v_ref[0, r, pl.ds(k_start, n_keys), :]
            if dilation == 1:
                rows = pl.ds(m0, Q_BLOCK)
            else:
                rows = pl.ds(m0 * dilation + r, Q_BLOCK, stride=dilation)
            pv, alpha, l_new = [], [], []
            for h in range(2):
                qh = jnp.where(in_head[h], q, jnp.zeros_like(q))
                s = lax.dot_general(qh, k, (((1,), (1,)), ((), ())), preferred_element_type=_F32) + bias
                m_blk = jnp.max(s, axis=-1, keepdims=True)
                if first:
                    m_new = jnp.broadcast_to(m_blk, (Q_BLOCK, LANES))
                else:
                    m_old = m_ref[h, rows, :]
                    m_new = jnp.maximum(m_old, m_blk)
                    alpha.append(jnp.exp(m_old - m_new))
                p = jnp.exp(s - jnp.concatenate([m_new] * (n_keys // LANES), axis=1))
                l_blk = jnp.sum(p, axis=-1, keepdims=True)
                if first:
                    l_new.append(jnp.broadcast_to(l_blk, (Q_BLOCK, LANES)))
                else:
                    l_new.append(alpha[h] * l_ref[h, rows, :] + l_blk)
                pv.append(jnp.dot(p.astype(_BF16), v, preferred_element_type=_F32))
                if not final:
                    m_ref[h, rows, :] = m_new
                    l_ref[h, rows, :] = l_new[h]
            acc = jnp.where(in_head[0], pv[0], pv[1])
            if not first:
                acc = acc + jnp.where(in_head[0], alpha[0], alpha[1]) * acc_ref[rows, :]
            if final:
                o_ref[0, rows, :] = (acc / jnp.where(in_head[0], l_new[0], l_new[1])).astype(o_ref.dtype)
            else:
                acc_ref[rows, :] = acc
            return carry

        lax.fori_loop(0, dilation * n_blocks, body, 0)

    run_group(q2, k2, v2, DILATED_GROUPS[2][1], True, False)
    run_group(q1, k1, v1, DILATED_GROUPS[1][1], False, False)
    run_group(q0, k0, v0, DILATED_GROUPS[0][1], False, True)


def _attention(qkv_groups, seq):
    b = qkv_groups[0].shape[0]
    d = D_MODEL
    n_pairs = d // LANES
    in_specs, args = [], []
    for qkv, (_, dil) in zip(qkv_groups, DILATED_GROUPS):
        for part in range(3):
            in_specs.append(pl.BlockSpec((1, dil, seq // dil, LANES),
                                         lambda i, p, part=part: (i, 0, 0, part * n_pairs + p)))
            args.append(qkv)
    wide = Q_BLOCK + 2 * HALF_WINDOW
    in_specs += [pl.BlockSpec((3, Q_BLOCK, wide), lambda i, p: (0, 0, 0)),
                 pl.BlockSpec((3, Q_BLOCK, Q_BLOCK), lambda i, p: (0, 0, 0))]
    args += [jnp.asarray(_band_bias(wide)), jnp.asarray(_band_bias(Q_BLOCK))]
    return pl.pallas_call(
        functools.partial(_attention_kernel, seq=seq),
        grid=(b, n_pairs),
        in_specs=in_specs,
        out_specs=pl.BlockSpec((1, seq, LANES), lambda i, p: (i, 0, p)),
        out_shape=jax.ShapeDtypeStruct((b, seq, d), _BF16),
        scratch_shapes=[pltpu.VMEM((2, seq, LANES), _F32),
                        pltpu.VMEM((2, seq, LANES), _F32),
                        pltpu.VMEM((seq, LANES), _F32)],
        compiler_params=_compiler_params(2),
        name="dilated_attention",
    )(*args)


FFN_CHUNK = MXU_WIDTH


def _ffn_block(x1, mod_ref, g_ref, w_in_ref, w_out_ref, act_ref):
    d, f = D_MODEL, FFN_HIDDEN
    h = _modulated_norm(x1, g_ref[...], mod_ref[:, 3 * d:4 * d], mod_ref[:, 4 * d:5 * d]).astype(_BF16)
    for c in range(f // FFN_CHUNK):
        lo = c * FFN_CHUNK
        gate = jnp.dot(h, w_in_ref[:, lo:lo + FFN_CHUNK], preferred_element_type=_F32)
        up = jnp.dot(h, w_in_ref[:, f + lo:f + lo + FFN_CHUNK], preferred_element_type=_F32)
        act_ref[:, lo:lo + FFN_CHUNK] = (gate * jax.nn.sigmoid(gate) * up).astype(_BF16)
    y = jnp.dot(act_ref[...], w_out_ref[...], preferred_element_type=_F32)
    return x1 + mod_ref[:, 5 * d:6 * d] * y


def _post_attention_kernel(x_ref, o_ref, mod_ref, w_o_ref, g_ref, w_in_ref, w_out_ref, out_ref, act_ref):
    d = D_MODEL
    y = jnp.dot(o_ref[0], w_o_ref[...], preferred_element_type=_F32)
    x1 = x_ref[0] + mod_ref[:, 2 * d:3 * d] * y
    out_ref[0] = _ffn_block(x1, mod_ref, g_ref, w_in_ref, w_out_ref, act_ref)


def _post_attention(x, o, mod, w_o, g, w_in, w_out):
    b, s, d = x.shape
    f = FFN_HIDDEN
    tm = 256
    return pl.pallas_call(
        _post_attention_kernel,
        grid=(b, s // tm),
        in_specs=[
            pl.BlockSpec((1, tm, d), lambda i, j: (i, j, 0)),
            pl.BlockSpec((1, tm, d), lambda i, j: (i, j, 0)),
            pl.BlockSpec((None, 1, 6 * d), lambda i, j: (i, 0, 0)),
            _resident((d, d), lambda i, j: (0, 0)),
            pl.BlockSpec((1, d), lambda i, j: (0, 0)),
            _resident((d, 2 * f), lambda i, j: (0, 0)),
            _resident((f, d), lambda i, j: (0, 0)),
        ],
        out_specs=pl.BlockSpec((1, tm, d), lambda i, j: (i, j, 0)),
        out_shape=jax.ShapeDtypeStruct((b, s, d), _F32),
        scratch_shapes=[pltpu.VMEM((tm, f), _BF16)],
        compiler_params=_compiler_params(2),
        name="post_attention",
    )(x, o, mod, w_o, g, w_in, w_out)


def _conv_in_kernel(x_ref, mod_ref, g_ref, w_ref, gate_ref, cu_ref):
    d = D_MODEL
    h = _modulated_norm(x_ref[0], g_ref[...], mod_ref[:, 0:d], mod_ref[:, d:2 * d]).astype(_BF16)
    for j in range(d // MXU_WIDTH):
        lo = j * MXU_WIDTH
        gate_ref[0, :, lo:lo + MXU_WIDTH] = jnp.dot(
            h, w_ref[:, lo:lo + MXU_WIDTH], preferred_element_type=_F32).astype(_BF16)
        c_gate = jnp.dot(h, w_ref[:, d + lo:d + lo + MXU_WIDTH], preferred_element_type=_F32)
        u = jnp.dot(h, w_ref[:, 2 * d + lo:2 * d + lo + MXU_WIDTH], preferred_element_type=_F32)
        cu_ref[0, :, lo:lo + MXU_WIDTH] = (c_gate * u).astype(_BF16)


def _conv_in(x, mod, g, w):
    b, s, d = x.shape
    tm = 512
    tile = pl.BlockSpec((1, tm, d), lambda i, j: (i, j, 0))
    return pl.pallas_call(
        _conv_in_kernel,
        grid=(b, s // tm),
        in_specs=[
            tile,
            pl.BlockSpec((None, 1, 6 * d), lambda i, j: (i, 0, 0)),
            pl.BlockSpec((1, d), lambda i, j: (0, 0)),
            _resident((d, 3 * d), lambda i, j: (0, 0)),
        ],
        out_specs=[tile, tile],
        out_shape=[jax.ShapeDtypeStruct((b, s, d), _BF16)] * 2,
        compiler_params=_compiler_params(2),
        name="conv_in",
    )(x, mod, g, w)


HALO_ROWS = 16


def _conv_out_kernel(x_ref, gate_ref, cu_ref, prev_ref, next_ref, mod_ref, cw_ref, w_o_ref, g_ref,
                     w_in_ref, w_out_ref, gf_ref, out_ref, act_ref, *, tm):
    d = D_MODEL
    j = pl.program_id(1)
    cu = cu_ref[0].astype(_F32)
    row = lax.broadcasted_iota(jnp.int32, (tm, d), 0)
    prev_row = jnp.where(j > 0, prev_ref[0, HALO_ROWS - 1:HALO_ROWS, :].astype(_F32), 0.0)
    next_row = jnp.where(j < pl.num_programs(1) - 1, next_ref[0, 0:1, :].astype(_F32), 0.0)
    before = jnp.where(row == 0, prev_row, pltpu.roll(cu, 1, 0))
    after = jnp.where(row == tm - 1, next_row, pltpu.roll(cu, tm - 1, 0))
    z = before * cw_ref[0:1, :] + cu * cw_ref[1:2, :] + after * cw_ref[2:3, :]
    gated = (gate_ref[0].astype(_F32) * z).astype(_BF16)
    y = jnp.dot(gated, w_o_ref[...], preferred_element_type=_F32)
    x1 = x_ref[0] + mod_ref[:, 2 * d:3 * d] * y
    x2 = _ffn_block(x1, mod_ref, g_ref, w_in_ref, w_out_ref, act_ref)
    out_ref[0] = _rms_norm(x2, gf_ref[...])


def _conv_out(x, gate, cu, mod, conv_w, w_o, g, w_in, w_out, final_g):
    b, s, d = x.shape
    f = FFN_HIDDEN
    tm = 256
    per_tile = tm // HALO_ROWS
    n_halo = s // HALO_ROWS
    tile = pl.BlockSpec((1, tm, d), lambda i, j: (i, j, 0))
    return pl.pallas_call(
        functools.partial(_conv_out_kernel, tm=tm),
        grid=(b, s // tm),
        in_specs=[
            tile, tile, tile,
            pl.BlockSpec((1, HALO_ROWS, d), lambda i, j: (i, jnp.maximum(j * per_tile - 1, 0), 0)),
            pl.BlockSpec((1, HALO_ROWS, d), lambda i, j: (i, jnp.minimum((j + 1) * per_tile, n_halo - 1), 0)),
            pl.BlockSpec((None, 1, 6 * d), lambda i, j: (i, 0, 0)),
            pl.BlockSpec((CONV_WIDTH, d), lambda i, j: (0, 0)),
            _resident((d, d), lambda i, j: (0, 0)),
            pl.BlockSpec((1, d), lambda i, j: (0, 0)),
            _resident((d, 2 * f), lambda i, j: (0, 0)),
            _resident((f, d), lambda i, j: (0, 0)),
            pl.BlockSpec((1, d), lambda i, j: (0, 0)),
        ],
        out_specs=tile,
        out_shape=jax.ShapeDtypeStruct((b, s, d), _F32),
        scratch_shapes=[pltpu.VMEM((tm, f), _BF16)],
        compiler_params=_compiler_params(2),
        name="conv_out",
    )(x, gate, cu, cu, cu, mod, conv_w, w_o, g, w_in, w_out, final_g)


def kernel(x, c, attn_w_qkv, attn_w_o, conv_w_in, conv_w, conv_w_out, ada_w, ada_b,
           norm_mix_g, norm_ffn_g, ffn_w_in, ffn_w_out, final_g):
    b, s, d = x.shape
    assert d == D_MODEL and attn_w_qkv.shape[0] == 1 and conv_w_in.shape[0] == 1 and ada_w.shape[0] == 2
    mod = _modulation(c, ada_w, ada_b).reshape(2, b, 1, 6 * d)
    bf = lambda w: w.astype(_BF16)

    w_qkv = bf(attn_w_qkv[0])
    qkv = [_qkv_group(x, mod[0], norm_mix_g[0:1], w_qkv[:, g * 3 * d:(g + 1) * 3 * d], dil)
           for g, (_, dil) in enumerate(DILATED_GROUPS)]
    o = _attention(qkv, s)
    x = _post_attention(x, o, mod[0], bf(attn_w_o[0]), norm_ffn_g[0:1], bf(ffn_w_in[0]), bf(ffn_w_out[0]))

    gate, cu = _conv_in(x, mod[1], norm_mix_g[1:2], bf(conv_w_in[0]))
    return _conv_out(x, gate, cu, mod[1], conv_w[0], bf(conv_w_out[0]), norm_ffn_g[1:2],
                     bf(ffn_w_in[1]), bf(ffn_w_out[1]), final_g.reshape(1, d))
```

```python
import functools

import numpy as np
import jax
import jax.numpy as jnp
from jax import lax
from jax.experimental import pallas as pl
from jax.experimental.pallas import tpu as pltpu

D_MODEL = 1024
HEAD_DIM = 64
N_HEADS = D_MODEL // HEAD_DIM
DILATED_GROUPS = ((128, 1), (512, 4), (2048, 16))
ROPE_THETA = 500000.0
ROPE_DIM = HEAD_DIM // 4
CONV_WIDTH = 3
FFN_HIDDEN = -(-8 * D_MODEL // (3 * 256)) * 256
NORM_EPS = 1e-6
NEG_INF = -1e30

LANES = 128
MXU_WIDTH = 256
HALF_WINDOW = 64
Q_BLOCK = 128
SCORE_SCALE = HEAD_DIM ** -0.5 * float(np.log2(np.e))
VMEM_LIMIT_BYTES = 56 * 1024 * 1024

assert all(w // (2 * d) == HALF_WINDOW for w, d in DILATED_GROUPS)

_F32 = jnp.float32
_BF16 = jnp.bfloat16


def _compiler_params(n_grid_dims):
    return pltpu.CompilerParams(
        dimension_semantics=("arbitrary",) * n_grid_dims,
        vmem_limit_bytes=VMEM_LIMIT_BYTES,
    )


def _resident(block_shape, index_map):
    return pl.BlockSpec(block_shape, index_map, pipeline_mode=pl.Buffered(1))


def _rms_norm(x, g):
    r = lax.rsqrt(jnp.mean(x * x, axis=-1, keepdims=True) + NORM_EPS)
    return (x * r) * g


def _modulated_norm(x, g, shift, scale):
    return _rms_norm(x, g) * (1.0 + scale) + shift


def _modulation_kernel(c_ref, w_ref, b_ref, o_ref):
    c = c_ref[...]
    cond = (c * jax.nn.sigmoid(c)).astype(_BF16)
    o_ref[...] = jnp.dot(cond, w_ref[...].astype(_BF16), preferred_element_type=_F32) + b_ref[...]


def _modulation(c, ada_w, ada_b):
    depth, d, n = ada_w.shape
    b = c.shape[0]
    tn = 1536
    return pl.pallas_call(
        _modulation_kernel,
        grid=(depth, n // tn),
        in_specs=[
            pl.BlockSpec((b, d), lambda i, j: (0, 0)),
            pl.BlockSpec((None, d, tn), lambda i, j: (i, 0, j)),
            pl.BlockSpec((None, 1, tn), lambda i, j: (i, 0, j)),
        ],
        out_specs=pl.BlockSpec((None, b, tn), lambda i, j: (i, 0, j)),
        out_shape=jax.ShapeDtypeStruct((depth, b, n), _F32),
        compiler_params=_compiler_params(2),
        name="modulation",
    )(c, ada_w, ada_b.reshape(depth, 1, n))


def _rope_tables(seq, dilation, q_scale):
    half = ROPE_DIM // 2
    length = seq // dilation
    inv = ROPE_THETA ** (-np.arange(half, dtype=np.float64) * (2.0 / ROPE_DIM))
    pos = (np.arange(length)[None, :] * dilation + np.arange(dilation)[:, None]).astype(np.float64)
    ang = pos[:, :, None] * inv[None, None, :]
    cos, sin = np.cos(ang), np.sin(ang)
    c = np.ones((dilation, length, HEAD_DIM))
    sa = np.zeros((dilation, length, HEAD_DIM))
    sb = np.zeros((dilation, length, HEAD_DIM))
    c[:, :, :half] = cos
    c[:, :, half:ROPE_DIM] = cos
    sa[:, :, :half] = -sin
    sb[:, :, half:ROPE_DIM] = sin
    k_tab = np.stack([np.tile(t, (1, 1, LANES // HEAD_DIM)) for t in (c, sa, sb)])
    return np.stack([k_tab * q_scale, k_tab]).astype(np.float32)


def _qkv_kernel(x_ref, mod_ref, g_ref, w_ref, tab_ref, o_ref, hs_ref, *, dilation, tm):
    d = D_MODEL
    x = x_ref[0]
    h = _modulated_norm(x, g_ref[...], mod_ref[:, 0:d], mod_ref[:, d:2 * d])
    rows = tm // dilation
    n_slabs = d // LANES
    if dilation == 1:
        hb = h.astype(_BF16)
    else:
        for c in range(n_slabs):
            hs_ref[c] = h[:, c * LANES:(c + 1) * LANES]
        hb = jnp.concatenate(
            [jnp.concatenate([hs_ref[c, pl.ds(r, rows, stride=dilation), :] for c in range(n_slabs)], axis=1)
             for r in range(dilation)], axis=0).astype(_BF16)
    for part in range(3):
        for j in range(d // MXU_WIDTH):
            col = part * d + j * MXU_WIDTH
            res = jnp.dot(hb, w_ref[:, col:col + MXU_WIDTH], preferred_element_type=_F32)
            if part < 2:
                halves = []
                for s in range(MXU_WIDTH // LANES):
                    xs = res[:, s * LANES:(s + 1) * LANES]
                    tabs = [jnp.concatenate([tab_ref[part, t, r] for r in range(dilation)], axis=0)
                            for t in range(3)]
                    halves.append(xs * tabs[0]
                                  + pltpu.roll(xs, LANES - ROPE_DIM // 2, 1) * tabs[1]
                                  + pltpu.roll(xs, ROPE_DIM // 2, 1) * tabs[2])
                res = jnp.concatenate(halves, axis=1)
            res = res.astype(_BF16)
            for r in range(dilation):
                o_ref[0, r, :, col:col + MXU_WIDTH] = res[r * rows:(r + 1) * rows]


def _qkv_group(x, mod, g, w, dilation):
    b, s, d = x.shape
    tm = 512
    length = s // dilation
    tabs = jnp.asarray(_rope_tables(s, dilation, SCORE_SCALE))
    kernel = functools.partial(_qkv_kernel, dilation=dilation, tm=tm)
    return pl.pallas_call(
        kernel,
        grid=(b, s // tm),
        in_specs=[
            pl.BlockSpec((1, tm, d), lambda i, j: (i, j, 0)),
            pl.BlockSpec((None, 1, 6 * d), lambda i, j: (i, 0, 0)),
            pl.BlockSpec((1, d), lambda i, j: (0, 0)),
            _resident((d, 3 * d), lambda i, j: (0, 0)),
            pl.BlockSpec((2, 3, dilation, tm // dilation, LANES), lambda i, j: (0, 0, 0, j, 0)),
        ],
        out_specs=pl.BlockSpec((1, dilation, tm // dilation, 3 * d), lambda i, j: (i, 0, j, 0)),
        out_shape=jax.ShapeDtypeStruct((b, dilation, length, 3 * d), _BF16),
        scratch_shapes=[pltpu.VMEM((d // LANES, tm, LANES), _F32)],
        compiler_params=_compiler_params(2),
        name=f"qkv_dil{dilation}",
    )(x, mod, g, w, tabs)


def _band_bias(n_keys):
    i = np.arange(Q_BLOCK)[:, None]
    j = np.arange(n_keys)[None, :]
    one = np.stack([np.where(np.abs(off + i - j) <= HALF_WINDOW, 0.0, NEG_INF)
                    for off in (0, HALF_WINDOW, 2 * HALF_WINDOW)])
    return np.concatenate([one, one], axis=1).astype(np.float32)


ATTN_UNROLL = 16


def _attention_kernel(q0, k0, v0, q1, k1, v1, q2, k2, v2, bias_w_ref, bias_n_ref, o_ref,
                      s4_ref, s1_ref, *, seq):
    lane = lax.broadcasted_iota(jnp.int32, (Q_BLOCK, LANES), 1)
    head_a = lane < HEAD_DIM
    d4 = DILATED_GROUPS[1][1]
    len4 = seq // d4

    def block_stats(q, k, v, bias):
        n_keys = k.shape[0]
        zero = jnp.zeros_like(q)
        q2 = jnp.concatenate([jnp.where(head_a, q, zero), jnp.where(head_a, zero, q)], axis=0)
        s = lax.dot_general(q2, k, (((1,), (1,)), ((), ())), preferred_element_type=_F32) + bias
        m = jnp.max(s, axis=-1, keepdims=True)
        p = jnp.exp2(s - m).astype(_BF16)
        v_ext = jnp.concatenate([v, jnp.ones((n_keys, LANES), _BF16)], axis=1)
        pv = jnp.dot(p, v_ext, preferred_element_type=_F32)
        m_pair = jnp.where(head_a, jnp.broadcast_to(m[:Q_BLOCK], (Q_BLOCK, LANES)),
                           jnp.broadcast_to(m[Q_BLOCK:], (Q_BLOCK, LANES)))
        l_pair = jnp.where(head_a, pv[:Q_BLOCK, LANES:], pv[Q_BLOCK:, LANES:])
        acc_pair = jnp.where(head_a, pv[:Q_BLOCK, :LANES], pv[Q_BLOCK:, :LANES])
        return m_pair, l_pair, acc_pair

    def merge(old, new):
        m = jnp.maximum(old[0], new[0])
        a_old = jnp.exp2(old[0] - m)
        a_new = jnp.exp2(new[0] - m)
        return m, a_old * old[1] + a_new * new[1], a_old * old[2] + a_new * new[2]

    def load_block(q_ref, k_ref, v_ref, r, j, length):
        n_keys = min(length, Q_BLOCK + 2 * HALF_WINDOW)
        m0 = pl.multiple_of(j * Q_BLOCK, Q_BLOCK)
        k_start = pl.multiple_of(jnp.clip(m0 - HALF_WINDOW, 0, length - n_keys), HALF_WINDOW)
        if n_keys == length:
            bias = bias_n_ref[0]
        else:
            bias = bias_w_ref[(m0 - k_start) // HALF_WINDOW]
        return (q_ref[0, r, pl.ds(m0, Q_BLOCK), :], k_ref[0, r, pl.ds(k_start, n_keys), :],
                v_ref[0, r, pl.ds(k_start, n_keys), :], bias)

    def group16(idx, carry):
        a, c = idx // d4, idx % d4
        state = block_stats(*load_block(q2, k2, v2, idx, 0, seq // DILATED_GROUPS[2][1]))
        rows = pl.ds(c * len4 + a, Q_BLOCK, stride=d4)
        for t in range(3):
            s4_ref[t, rows, :] = state[t]
        return carry

    def group4(idx, carry):
        c, j = idx // (len4 // Q_BLOCK), idx % (len4 // Q_BLOCK)
        new = block_stats(*load_block(q1, k1, v1, c, j, len4))
        src = pl.ds(pl.multiple_of(c * len4 + j * Q_BLOCK, Q_BLOCK), Q_BLOCK)
        state = merge([s4_ref[t, src, :] for t in range(3)], new)
        rows = pl.ds(j * Q_BLOCK * d4 + c, Q_BLOCK, stride=d4)
        for t in range(3):
            s1_ref[t, rows, :] = state[t]
        return carry

    def group1(j, carry):
        new = block_stats(*load_block(q0, k0, v0, 0, j, seq))
        rows = pl.ds(pl.multiple_of(j * Q_BLOCK, Q_BLOCK), Q_BLOCK)
        _, l, acc = merge([s1_ref[t, rows, :] for t in range(3)], new)
        o_ref[0, rows, :] = (acc / l).astype(o_ref.dtype)
        return carry

    n_blocks = seq // Q_BLOCK
    lax.fori_loop(0, n_blocks, group16, 0, unroll=ATTN_UNROLL)
    lax.fori_loop(0, n_blocks, group4, 0, unroll=ATTN_UNROLL)
    lax.fori_loop(0, n_blocks, group1, 0, unroll=ATTN_UNROLL)


def _attention(qkv_groups, seq):
    b = qkv_groups[0].shape[0]
    d = D_MODEL
    n_pairs = d // LANES
    in_specs, args = [], []
    for qkv, (_, dil) in zip(qkv_groups, DILATED_GROUPS):
        for part in range(3):
            in_specs.append(pl.BlockSpec((1, dil, seq // dil, LANES),
                                         lambda i, p, part=part: (i, 0, 0, part * n_pairs + p)))
            args.append(qkv)
    wide = Q_BLOCK + 2 * HALF_WINDOW
    in_specs += [pl.BlockSpec((3, 2 * Q_BLOCK, wide), lambda i, p: (0, 0, 0)),
                 pl.BlockSpec((3, 2 * Q_BLOCK, Q_BLOCK), lambda i, p: (0, 0, 0))]
    args += [jnp.asarray(_band_bias(wide)), jnp.asarray(_band_bias(Q_BLOCK))]
    return pl.pallas_call(
        functools.partial(_attention_kernel, seq=seq),
        grid=(b, n_pairs),
        in_specs=in_specs,
        out_specs=pl.BlockSpec((1, seq, LANES), lambda i, p: (i, 0, p)),
        out_shape=jax.ShapeDtypeStruct((b, seq, d), _BF16),
        scratch_shapes=[pltpu.VMEM((3, seq, LANES), _F32),
                        pltpu.VMEM((3, seq, LANES), _F32)],
        compiler_params=_compiler_params(2),
        name="dilated_attention",
    )(*args)


FFN_CHUNK = MXU_WIDTH


def _ffn_block(x1, mod_ref, g_ref, w_in_ref, w_out_ref, act_ref):
    d, f = D_MODEL, FFN_HIDDEN
    h = _modulated_norm(x1, g_ref[...], mod_ref[:, 3 * d:4 * d], mod_ref[:, 4 * d:5 * d]).astype(_BF16)
    for c in range(f // FFN_CHUNK):
        lo = c * FFN_CHUNK
        gate = jnp.dot(h, w_in_ref[:, lo:lo + FFN_CHUNK], preferred_element_type=_F32)
        up = jnp.dot(h, w_in_ref[:, f + lo:f + lo + FFN_CHUNK], preferred_element_type=_F32)
        act_ref[:, lo:lo + FFN_CHUNK] = (gate * jax.nn.sigmoid(gate) * up).astype(_BF16)
    y = jnp.dot(act_ref[...], w_out_ref[...], preferred_element_type=_F32)
    return x1 + mod_ref[:, 5 * d:6 * d] * y


def _post_attention_kernel(x_ref, o_ref, mod_ref, w_o_ref, g_ref, w_in_ref, w_out_ref, out_ref, act_ref):
    d = D_MODEL
    y = jnp.dot(o_ref[0], w_o_ref[...], preferred_element_type=_F32)
    x1 = x_ref[0] + mod_ref[:, 2 * d:3 * d] * y
    out_ref[0] = _ffn_block(x1, mod_ref, g_ref, w_in_ref, w_out_ref, act_ref)


def _post_attention(x, o, mod, w_o, g, w_in, w_out):
    b, s, d = x.shape
    f = FFN_HIDDEN
    tm = 256
    return pl.pallas_call(
        _post_attention_kernel,
        grid=(b, s // tm),
        in_specs=[
            pl.BlockSpec((1, tm, d), lambda i, j: (i, j, 0)),
            pl.BlockSpec((1, tm, d), lambda i, j: (i, j, 0)),
            pl.BlockSpec((None, 1, 6 * d), lambda i, j: (i, 0, 0)),
            _resident((d, d), lambda i, j: (0, 0)),
            pl.BlockSpec((1, d), lambda i, j: (0, 0)),
            _resident((d, 2 * f), lambda i, j: (0, 0)),
            _resident((f, d), lambda i, j: (0, 0)),
        ],
        out_specs=pl.BlockSpec((1, tm, d), lambda i, j: (i, j, 0)),
        out_shape=jax.ShapeDtypeStruct((b, s, d), _F32),
        scratch_shapes=[pltpu.VMEM((tm, f), _BF16)],
        compiler_params=_compiler_params(2),
        name="post_attention",
    )(x, o, mod, w_o, g, w_in, w_out)


def _conv_in_kernel(x_ref, mod_ref, g_ref, w_ref, gate_ref, cu_ref):
    d = D_MODEL
    h = _modulated_norm(x_ref[0], g_ref[...], mod_ref[:, 0:d], mod_ref[:, d:2 * d]).astype(_BF16)
    for j in range(d // MXU_WIDTH):
        lo = j * MXU_WIDTH
        gate_ref[0, :, lo:lo + MXU_WIDTH] = jnp.dot(
            h, w_ref[:, lo:lo + MXU_WIDTH], preferred_element_type=_F32).astype(_BF16)
        c_gate = jnp.dot(h, w_ref[:, d + lo:d + lo + MXU_WIDTH], preferred_element_type=_F32)
        u = jnp.dot(h, w_ref[:, 2 * d + lo:2 * d + lo + MXU_WIDTH], preferred_element_type=_F32)
        cu_ref[0, :, lo:lo + MXU_WIDTH] = (c_gate * u).astype(_BF16)


def _conv_in(x, mod, g, w):
    b, s, d = x.shape
    tm = 512
    tile = pl.BlockSpec((1, tm, d), lambda i, j: (i, j, 0))
    return pl.pallas_call(
        _conv_in_kernel,
        grid=(b, s // tm),
        in_specs=[
            tile,
            pl.BlockSpec((None, 1, 6 * d), lambda i, j: (i, 0, 0)),
            pl.BlockSpec((1, d), lambda i, j: (0, 0)),
            _resident((d, 3 * d), lambda i, j: (0, 0)),
        ],
        out_specs=[tile, tile],
        out_shape=[jax.ShapeDtypeStruct((b, s, d), _BF16)] * 2,
        compiler_params=_compiler_params(2),
        name="conv_in",
    )(x, mod, g, w)


HALO_ROWS = 16


def _conv_out_kernel(x_ref, gate_ref, cu_ref, prev_ref, next_ref, mod_ref, cw_ref, w_o_ref, g_ref,
                     w_in_ref, w_out_ref, gf_ref, out_ref, act_ref, *, tm):
    d = D_MODEL
    j = pl.program_id(1)
    cu = cu_ref[0].astype(_F32)
    row = lax.broadcasted_iota(jnp.int32, (tm, d), 0)
    prev_row = jnp.where(j > 0, prev_ref[0, HALO_ROWS - 1:HALO_ROWS, :].astype(_F32), 0.0)
    next_row = jnp.where(j < pl.num_programs(1) - 1, next_ref[0, 0:1, :].astype(_F32), 0.0)
    before = jnp.where(row == 0, prev_row, pltpu.roll(cu, 1, 0))
    after = jnp.where(row == tm - 1, next_row, pltpu.roll(cu, tm - 1, 0))
    z = before * cw_ref[0:1, :] + cu * cw_ref[1:2, :] + after * cw_ref[2:3, :]
    gated = (gate_ref[0].astype(_F32) * z).astype(_BF16)
    y = jnp.dot(gated, w_o_ref[...], preferred_element_type=_F32)
    x1 = x_ref[0] + mod_ref[:, 2 * d:3 * d] * y
    x2 = _ffn_block(x1, mod_ref, g_ref, w_in_ref, w_out_ref, act_ref)
    out_ref[0] = _rms_norm(x2, gf_ref[...])


def _conv_out(x, gate, cu, mod, conv_w, w_o, g, w_in, w_out, final_g):
    b, s, d = x.shape
    f = FFN_HIDDEN
    tm = 256
    per_tile = tm // HALO_ROWS
    n_halo = s // HALO_ROWS
    tile = pl.BlockSpec((1, tm, d), lambda i, j: (i, j, 0))
    return pl.pallas_call(
        functools.partial(_conv_out_kernel, tm=tm),
        grid=(b, s // tm),
        in_specs=[
            tile, tile, tile,
            pl.BlockSpec((1, HALO_ROWS, d), lambda i, j: (i, jnp.maximum(j * per_tile - 1, 0), 0)),
            pl.BlockSpec((1, HALO_ROWS, d), lambda i, j: (i, jnp.minimum((j + 1) * per_tile, n_halo - 1), 0)),
            pl.BlockSpec((None, 1, 6 * d), lambda i, j: (i, 0, 0)),
            pl.BlockSpec((CONV_WIDTH, d), lambda i, j: (0, 0)),
            _resident((d, d), lambda i, j: (0, 0)),
            pl.BlockSpec((1, d), lambda i, j: (0, 0)),
            _resident((d, 2 * f), lambda i, j: (0, 0)),
            _resident((f, d), lambda i, j: (0, 0)),
            pl.BlockSpec((1, d), lambda i, j: (0, 0)),
        ],
        out_specs=tile,
        out_shape=jax.ShapeDtypeStruct((b, s, d), _F32),
        scratch_shapes=[pltpu.VMEM((tm, f), _BF16)],
        compiler_params=_compiler_params(2),
        name="conv_out",
    )(x, gate, cu, cu, cu, mod, conv_w, w_o, g, w_in, w_out, final_g)


def kernel(x, c, attn_w_qkv, attn_w_o, conv_w_in, conv_w, conv_w_out, ada_w, ada_b,
           norm_mix_g, norm_ffn_g, ffn_w_in, ffn_w_out, final_g):
    b, s, d = x.shape
    assert d == D_MODEL and attn_w_qkv.shape[0] == 1 and conv_w_in.shape[0] == 1 and ada_w.shape[0] == 2
    mod = _modulation(c, ada_w, ada_b).reshape(2, b, 1, 6 * d)
    bf = lambda w: w.astype(_BF16)

    w_qkv = bf(attn_w_qkv[0])
    qkv = [_qkv_group(x, mod[0], norm_mix_g[0:1], w_qkv[:, g * 3 * d:(g + 1) * 3 * d], dil)
           for g, (_, dil) in enumerate(DILATED_GROUPS)]
    o = _attention(qkv, s)
    x = _post_attention(x, o, mod[0], bf(attn_w_o[0]), norm_ffn_g[0:1], bf(ffn_w_in[0]), bf(ffn_w_out[0]))

    gate, cu = _conv_in(x, mod[1], norm_mix_g[1:2], bf(conv_w_in[0]))
    return _conv_out(x, gate, cu, mod[1], conv_w[0], bf(conv_w_out[0]), norm_ffn_g[1:2],
                     bf(ffn_w_in[1]), bf(ffn_w_out[1]), final_g.reshape(1, d))
```

```python
import functools

import numpy as np
import jax
import jax.numpy as jnp
from jax import lax
from jax.experimental import pallas as pl
from jax.experimental.pallas import tpu as pltpu

D_MODEL = 1024
HEAD_DIM = 64
N_HEADS = D_MODEL // HEAD_DIM
DILATED_GROUPS = ((128, 1), (512, 4), (2048, 16))
ROPE_THETA = 500000.0
ROPE_DIM = HEAD_DIM // 4
CONV_WIDTH = 3
FFN_HIDDEN = -(-8 * D_MODEL // (3 * 256)) * 256
NORM_EPS = 1e-6
NEG_INF = -1e30

LANES = 128
MXU_WIDTH = 256
HALF_WINDOW = 64
Q_BLOCK = 128
SCORE_SCALE = HEAD_DIM ** -0.5 * float(np.log2(np.e))
VMEM_LIMIT_BYTES = 56 * 1024 * 1024

assert all(w // (2 * d) == HALF_WINDOW for w, d in DILATED_GROUPS)

_F32 = jnp.float32
_BF16 = jnp.bfloat16


def _compiler_params(n_grid_dims):
    return pltpu.CompilerParams(
        dimension_semantics=("arbitrary",) * n_grid_dims,
        vmem_limit_bytes=VMEM_LIMIT_BYTES,
    )


def _resident(block_shape, index_map):
    return pl.BlockSpec(block_shape, index_map, pipeline_mode=pl.Buffered(1))


def _rms_norm(x, g):
    r = lax.rsqrt(jnp.mean(x * x, axis=-1, keepdims=True) + NORM_EPS)
    return (x * r) * g


def _modulated_norm(x, g, shift, scale):
    return _rms_norm(x, g) * (1.0 + scale) + shift


def _modulation_kernel(c_ref, w_ref, b_ref, o_ref):
    c = c_ref[...]
    cond = (c * jax.nn.sigmoid(c)).astype(_BF16)
    o_ref[...] = jnp.dot(cond, w_ref[...].astype(_BF16), preferred_element_type=_F32) + b_ref[...]


def _modulation(c, ada_w, ada_b):
    depth, d, n = ada_w.shape
    b = c.shape[0]
    tn = 1536
    return pl.pallas_call(
        _modulation_kernel,
        grid=(depth, n // tn),
        in_specs=[
            pl.BlockSpec((b, d), lambda i, j: (0, 0)),
            pl.BlockSpec((None, d, tn), lambda i, j: (i, 0, j)),
            pl.BlockSpec((None, 1, tn), lambda i, j: (i, 0, j)),
        ],
        out_specs=pl.BlockSpec((None, b, tn), lambda i, j: (i, 0, j)),
        out_shape=jax.ShapeDtypeStruct((depth, b, n), _F32),
        compiler_params=_compiler_params(2),
        name="modulation",
    )(c, ada_w, ada_b.reshape(depth, 1, n))


def _rope_tables(seq, dilation, q_scale):
    half = ROPE_DIM // 2
    length = seq // dilation
    inv = ROPE_THETA ** (-np.arange(half, dtype=np.float64) * (2.0 / ROPE_DIM))
    pos = (np.arange(length)[None, :] * dilation + np.arange(dilation)[:, None]).astype(np.float64)
    ang = pos[:, :, None] * inv[None, None, :]
    cos, sin = np.cos(ang), np.sin(ang)
    c = np.ones((dilation, length, HEAD_DIM))
    sa = np.zeros((dilation, length, HEAD_DIM))
    sb = np.zeros((dilation, length, HEAD_DIM))
    c[:, :, :half] = cos
    c[:, :, half:ROPE_DIM] = cos
    sa[:, :, :half] = -sin
    sb[:, :, half:ROPE_DIM] = sin
    k_tab = np.stack([np.tile(t, (1, 1, LANES // HEAD_DIM)) for t in (c, sa, sb)])
    return np.stack([k_tab * q_scale, k_tab]).astype(np.float32)


def _qkv_kernel(x_ref, mod_ref, g_ref, w_ref, tab_ref, o_ref, hs_ref, *, dilation, tm):
    d = D_MODEL
    x = x_ref[0]
    h = _modulated_norm(x, g_ref[...], mod_ref[:, 0:d], mod_ref[:, d:2 * d])
    rows = tm // dilation
    n_slabs = d // LANES
    if dilation == 1:
        hb = h.astype(_BF16)
    else:
        for c in range(n_slabs):
            hs_ref[c] = h[:, c * LANES:(c + 1) * LANES]
        hb = jnp.concatenate(
            [jnp.concatenate([hs_ref[c, pl.ds(r, rows, stride=dilation), :] for c in range(n_slabs)], axis=1)
             for r in range(dilation)], axis=0).astype(_BF16)
    for part in range(3):
        for j in range(d // MXU_WIDTH):
            col = part * d + j * MXU_WIDTH
            res = jnp.dot(hb, w_ref[:, col:col + MXU_WIDTH], preferred_element_type=_F32)
            if part < 2:
                halves = []
                for s in range(MXU_WIDTH // LANES):
                    xs = res[:, s * LANES:(s + 1) * LANES]
                    tabs = [jnp.concatenate([tab_ref[part, t, r] for r in range(dilation)], axis=0)
                            for t in range(3)]
                    halves.append(xs * tabs[0]
                                  + pltpu.roll(xs, LANES - ROPE_DIM // 2, 1) * tabs[1]
                                  + pltpu.roll(xs, ROPE_DIM // 2, 1) * tabs[2])
                res = jnp.concatenate(halves, axis=1)
            res = res.astype(_BF16)
            for r in range(dilation):
                o_ref[0, r, :, col:col + MXU_WIDTH] = res[r * rows:(r + 1) * rows]


def _qkv_group(x, mod, g, w, group, dilation):
    b, s, d = x.shape
    tm = 512
    length = s // dilation
    tabs = jnp.asarray(_rope_tables(s, dilation, SCORE_SCALE))
    kernel = functools.partial(_qkv_kernel, dilation=dilation, tm=tm)
    return pl.pallas_call(
        kernel,
        grid=(b, s // tm),
        in_specs=[
            pl.BlockSpec((1, tm, d), lambda i, j: (i, j, 0)),
            pl.BlockSpec((None, 1, 6 * d), lambda i, j: (i, 0, 0)),
            pl.BlockSpec((1, d), lambda i, j: (0, 0)),
            _resident((d, 3 * d), lambda i, j: (0, group)),
            pl.BlockSpec((2, 3, dilation, tm // dilation, LANES), lambda i, j: (0, 0, 0, j, 0)),
        ],
        out_specs=pl.BlockSpec((1, dilation, tm // dilation, 3 * d), lambda i, j: (i, 0, j, 0)),
        out_shape=jax.ShapeDtypeStruct((b, dilation, length, 3 * d), _BF16),
        scratch_shapes=[pltpu.VMEM((d // LANES, tm, LANES), _F32)],
        compiler_params=_compiler_params(2),
        name=f"qkv_dil{dilation}",
    )(x, mod, g, w, tabs)


def _band_bias(n_keys):
    i = np.arange(Q_BLOCK)[:, None]
    j = np.arange(n_keys)[None, :]
    one = np.stack([np.where(np.abs(off + i - j) <= HALF_WINDOW, 0.0, NEG_INF)
                    for off in (0, HALF_WINDOW, 2 * HALF_WINDOW)])
    return np.concatenate([one, one], axis=1).astype(np.float32)


ATTN_UNROLL = 16


def _attention_kernel(q0, k0, v0, q1, k1, v1, q2, k2, v2, bias_w_ref, bias_n_ref, o_ref,
                      s4_ref, s1_ref, *, seq):
    lane = lax.broadcasted_iota(jnp.int32, (Q_BLOCK, LANES), 1)
    head_a = lane < HEAD_DIM
    d4 = DILATED_GROUPS[1][1]
    len4 = seq // d4

    def block_stats(q, k, v, bias):
        n_keys = k.shape[0]
        zero = jnp.zeros_like(q)
        q2 = jnp.concatenate([jnp.where(head_a, q, zero), jnp.where(head_a, zero, q)], axis=0)
        s = lax.dot_general(q2, k, (((1,), (1,)), ((), ())), preferred_element_type=_F32) + bias
        m = jnp.max(s, axis=-1, keepdims=True)
        p = jnp.exp2(s - m).astype(_BF16)
        v_ext = jnp.concatenate([v, jnp.ones((n_keys, LANES), _BF16)], axis=1)
        pv = jnp.dot(p, v_ext, preferred_element_type=_F32)
        m_pair = jnp.where(head_a, jnp.broadcast_to(m[:Q_BLOCK], (Q_BLOCK, LANES)),
                           jnp.broadcast_to(m[Q_BLOCK:], (Q_BLOCK, LANES)))
        l_pair = jnp.where(head_a, pv[:Q_BLOCK, LANES:], pv[Q_BLOCK:, LANES:])
        acc_pair = jnp.where(head_a, pv[:Q_BLOCK, :LANES], pv[Q_BLOCK:, :LANES])
        return m_pair, l_pair, acc_pair

    def merge(old, new):
        m = jnp.maximum(old[0], new[0])
        a_old = jnp.exp2(old[0] - m)
        a_new = jnp.exp2(new[0] - m)
        return m, a_old * old[1] + a_new * new[1], a_old * old[2] + a_new * new[2]

    def load_block(q_ref, k_ref, v_ref, r, j, length):
        n_keys = min(length, Q_BLOCK + 2 * HALF_WINDOW)
        m0 = pl.multiple_of(j * Q_BLOCK, Q_BLOCK)
        k_start = pl.multiple_of(jnp.clip(m0 - HALF_WINDOW, 0, length - n_keys), HALF_WINDOW)
        if n_keys == length:
            bias = bias_n_ref[0]
        else:
            bias = bias_w_ref[(m0 - k_start) // HALF_WINDOW]
        return (q_ref[0, r, pl.ds(m0, Q_BLOCK), :], k_ref[0, r, pl.ds(k_start, n_keys), :],
                v_ref[0, r, pl.ds(k_start, n_keys), :], bias)

    def group16(idx, carry):
        a, c = idx // d4, idx % d4
        state = block_stats(*load_block(q2, k2, v2, idx, 0, seq // DILATED_GROUPS[2][1]))
        rows = pl.ds(c * len4 + a, Q_BLOCK, stride=d4)
        for t in range(3):
            s4_ref[t, rows, :] = state[t]
        return carry

    def group4(idx, carry):
        c, j = idx // (len4 // Q_BLOCK), idx % (len4 // Q_BLOCK)
        new = block_stats(*load_block(q1, k1, v1, c, j, len4))
        src = pl.ds(pl.multiple_of(c * len4 + j * Q_BLOCK, Q_BLOCK), Q_BLOCK)
        state = merge([s4_ref[t, src, :] for t in range(3)], new)
        rows = pl.ds(j * Q_BLOCK * d4 + c, Q_BLOCK, stride=d4)
        for t in range(3):
            s1_ref[t, rows, :] = state[t]
        return carry

    def group1(j, carry):
        new = block_stats(*load_block(q0, k0, v0, 0, j, seq))
        rows = pl.ds(pl.multiple_of(j * Q_BLOCK, Q_BLOCK), Q_BLOCK)
        _, l, acc = merge([s1_ref[t, rows, :] for t in range(3)], new)
        o_ref[0, rows, :] = (acc / l).astype(o_ref.dtype)
        return carry

    n_blocks = seq // Q_BLOCK
    lax.fori_loop(0, n_blocks, group16, 0, unroll=ATTN_UNROLL)
    lax.fori_loop(0, n_blocks, group4, 0, unroll=ATTN_UNROLL)
    lax.fori_loop(0, n_blocks, group1, 0, unroll=ATTN_UNROLL)


def _attention(qkv_groups, seq):
    b = qkv_groups[0].shape[0]
    d = D_MODEL
    n_pairs = d // LANES
    in_specs, args = [], []
    for qkv, (_, dil) in zip(qkv_groups, DILATED_GROUPS):
        for part in range(3):
            in_specs.append(pl.BlockSpec((1, dil, seq // dil, LANES),
                                         lambda i, p, part=part: (i, 0, 0, part * n_pairs + p)))
            args.append(qkv)
    wide = Q_BLOCK + 2 * HALF_WINDOW
    in_specs += [pl.BlockSpec((3, 2 * Q_BLOCK, wide), lambda i, p: (0, 0, 0)),
                 pl.BlockSpec((3, 2 * Q_BLOCK, Q_BLOCK), lambda i, p: (0, 0, 0))]
    args += [jnp.asarray(_band_bias(wide)), jnp.asarray(_band_bias(Q_BLOCK))]
    return pl.pallas_call(
        functools.partial(_attention_kernel, seq=seq),
        grid=(b, n_pairs),
        in_specs=in_specs,
        out_specs=pl.BlockSpec((1, seq, LANES), lambda i, p: (i, 0, p)),
        out_shape=jax.ShapeDtypeStruct((b, seq, d), _BF16),
        scratch_shapes=[pltpu.VMEM((3, seq, LANES), _F32),
                        pltpu.VMEM((3, seq, LANES), _F32)],
        compiler_params=_compiler_params(2),
        name="dilated_attention",
    )(*args)


FFN_CHUNK = MXU_WIDTH


TAIL_ROWS = 512
TAIL_SUB_ROWS = 256


def _ffn_block(x1, mod_ref, g_ref, w_in_ref, w_out_ref, act_ref, rows):
    d, f = D_MODEL, FFN_HIDDEN
    h = _modulated_norm(x1, g_ref[...], mod_ref[:, 3 * d:4 * d], mod_ref[:, 4 * d:5 * d]).astype(_BF16)
    for c in range(f // FFN_CHUNK):
        lo = c * FFN_CHUNK
        gate = jnp.dot(h, w_in_ref[:, lo:lo + FFN_CHUNK], preferred_element_type=_F32)
        up = jnp.dot(h, w_in_ref[:, f + lo:f + lo + FFN_CHUNK], preferred_element_type=_F32)
        act_ref[rows, lo:lo + FFN_CHUNK] = (gate * jax.nn.sigmoid(gate) * up).astype(_BF16)
    y = jnp.dot(act_ref[rows, :], w_out_ref[...], preferred_element_type=_F32)
    return x1 + mod_ref[:, 5 * d:6 * d] * y


def _post_attention_kernel(x_ref, o_ref, mod_ref, w_o_ref, g_ref, w_in_ref, w_out_ref, out_ref, act_ref):
    d = D_MODEL
    for sub in range(TAIL_ROWS // TAIL_SUB_ROWS):
        rows = slice(sub * TAIL_SUB_ROWS, (sub + 1) * TAIL_SUB_ROWS)
        y = jnp.dot(o_ref[0, rows, :], w_o_ref[...], preferred_element_type=_F32)
        x1 = x_ref[0, rows, :] + mod_ref[:, 2 * d:3 * d] * y
        out_ref[0, rows, :] = _ffn_block(x1, mod_ref, g_ref, w_in_ref, w_out_ref, act_ref, rows)


def _post_attention(x, o, mod, w_o, g, w_in, w_out):
    b, s, d = x.shape
    f = FFN_HIDDEN
    tm = TAIL_ROWS
    return pl.pallas_call(
        _post_attention_kernel,
        grid=(b, s // tm),
        in_specs=[
            pl.BlockSpec((1, tm, d), lambda i, j: (i, j, 0)),
            pl.BlockSpec((1, tm, d), lambda i, j: (i, j, 0)),
            pl.BlockSpec((None, 1, 6 * d), lambda i, j: (i, 0, 0)),
            _resident((d, d), lambda i, j: (0, 0)),
            pl.BlockSpec((1, d), lambda i, j: (0, 0)),
            _resident((d, 2 * f), lambda i, j: (0, 0)),
            _resident((f, d), lambda i, j: (0, 0)),
        ],
        out_specs=pl.BlockSpec((1, tm, d), lambda i, j: (i, j, 0)),
        out_shape=jax.ShapeDtypeStruct((b, s, d), _F32),
        scratch_shapes=[pltpu.VMEM((tm, f), _BF16)],
        compiler_params=_compiler_params(2),
        name="post_attention",
    )(x, o, mod, w_o, g, w_in, w_out)


def _conv_in_kernel(x_ref, mod_ref, g_ref, w_ref, gate_ref, cu_ref):
    d = D_MODEL
    h = _modulated_norm(x_ref[0], g_ref[...], mod_ref[:, 0:d], mod_ref[:, d:2 * d]).astype(_BF16)
    for j in range(d // MXU_WIDTH):
        lo = j * MXU_WIDTH
        gate_ref[0, :, lo:lo + MXU_WIDTH] = jnp.dot(
            h, w_ref[:, lo:lo + MXU_WIDTH], preferred_element_type=_F32).astype(_BF16)
        c_gate = jnp.dot(h, w_ref[:, d + lo:d + lo + MXU_WIDTH], preferred_element_type=_F32)
        u = jnp.dot(h, w_ref[:, 2 * d + lo:2 * d + lo + MXU_WIDTH], preferred_element_type=_F32)
        cu_ref[0, :, lo:lo + MXU_WIDTH] = (c_gate * u).astype(_BF16)


def _conv_in(x, mod, g, w):
    b, s, d = x.shape
    tm = 512
    tile = pl.BlockSpec((1, tm, d), lambda i, j: (i, j, 0))
    return pl.pallas_call(
        _conv_in_kernel,
        grid=(b, s // tm),
        in_specs=[
            tile,
            pl.BlockSpec((None, 1, 6 * d), lambda i, j: (i, 0, 0)),
            pl.BlockSpec((1, d), lambda i, j: (0, 0)),
            _resident((d, 3 * d), lambda i, j: (0, 0)),
        ],
        out_specs=[tile, tile],
        out_shape=[jax.ShapeDtypeStruct((b, s, d), _BF16)] * 2,
        compiler_params=_compiler_params(2),
        name="conv_in",
    )(x, mod, g, w)


HALO_ROWS = 16


def _conv_out_kernel(x_ref, gate_ref, cu_ref, prev_ref, next_ref, mod_ref, cw_ref, w_o_ref, g_ref,
                     w_in_ref, w_out_ref, gf_ref, out_ref, act_ref, *, tm):
    d = D_MODEL
    j = pl.program_id(1)
    cu = cu_ref[0].astype(_F32)
    row = lax.broadcasted_iota(jnp.int32, (tm, d), 0)
    prev_row = jnp.where(j > 0, prev_ref[0, HALO_ROWS - 1:HALO_ROWS, :].astype(_F32), 0.0)
    next_row = jnp.where(j < pl.num_programs(1) - 1, next_ref[0, 0:1, :].astype(_F32), 0.0)
    before = jnp.where(row == 0, prev_row, pltpu.roll(cu, 1, 0))
    after = jnp.where(row == tm - 1, next_row, pltpu.roll(cu, tm - 1, 0))
    z = before * cw_ref[0:1, :] + cu * cw_ref[1:2, :] + after * cw_ref[2:3, :]
    gated = (gate_ref[0].astype(_F32) * z).astype(_BF16)
    for sub in range(tm // TAIL_SUB_ROWS):
        rows = slice(sub * TAIL_SUB_ROWS, (sub + 1) * TAIL_SUB_ROWS)
        y = jnp.dot(gated[rows], w_o_ref[...], preferred_element_type=_F32)
        x1 = x_ref[0, rows, :] + mod_ref[:, 2 * d:3 * d] * y
        x2 = _ffn_block(x1, mod_ref, g_ref, w_in_ref, w_out_ref, act_ref, rows)
        out_ref[0, rows, :] = _rms_norm(x2, gf_ref[...])


def _conv_out(x, gate, cu, mod, conv_w, w_o, g, w_in, w_out, final_g):
    b, s, d = x.shape
    f = FFN_HIDDEN
    tm = TAIL_ROWS
    per_tile = tm // HALO_ROWS
    n_halo = s // HALO_ROWS
    tile = pl.BlockSpec((1, tm, d), lambda i, j: (i, j, 0))
    return pl.pallas_call(
        functools.partial(_conv_out_kernel, tm=tm),
        grid=(b, s // tm),
        in_specs=[
            tile, tile, tile,
            pl.BlockSpec((1, HALO_ROWS, d), lambda i, j: (i, jnp.maximum(j * per_tile - 1, 0), 0)),
            pl.BlockSpec((1, HALO_ROWS, d), lambda i, j: (i, jnp.minimum((j + 1) * per_tile, n_halo - 1), 0)),
            pl.BlockSpec((None, 1, 6 * d), lambda i, j: (i, 0, 0)),
            pl.BlockSpec((CONV_WIDTH, d), lambda i, j: (0, 0)),
            _resident((d, d), lambda i, j: (0, 0)),
            pl.BlockSpec((1, d), lambda i, j: (0, 0)),
            _resident((d, 2 * f), lambda i, j: (0, 0)),
            _resident((f, d), lambda i, j: (0, 0)),
            pl.BlockSpec((1, d), lambda i, j: (0, 0)),
        ],
        out_specs=tile,
        out_shape=jax.ShapeDtypeStruct((b, s, d), _F32),
        scratch_shapes=[pltpu.VMEM((tm, f), _BF16)],
        compiler_params=_compiler_params(2),
        name="conv_out",
    )(x, gate, cu, cu, cu, mod, conv_w, w_o, g, w_in, w_out, final_g)


def kernel(x, c, attn_w_qkv, attn_w_o, conv_w_in, conv_w, conv_w_out, ada_w, ada_b,
           norm_mix_g, norm_ffn_g, ffn_w_in, ffn_w_out, final_g):
    b, s, d = x.shape
    assert d == D_MODEL and attn_w_qkv.shape[0] == 1 and conv_w_in.shape[0] == 1 and ada_w.shape[0] == 2
    mod = _modulation(c, ada_w, ada_b).reshape(2, b, 1, 6 * d)
    bf = lambda w: w.astype(_BF16)

    w_qkv = bf(attn_w_qkv[0])
    qkv = [_qkv_group(x, mod[0], norm_mix_g[0:1], w_qkv, g, dil)
           for g, (_, dil) in enumerate(DILATED_GROUPS)]
    o = _attention(qkv, s)
    x = _post_attention(x, o, mod[0], bf(attn_w_o[0]), norm_ffn_g[0:1], bf(ffn_w_in[0]), bf(ffn_w_out[0]))

    gate, cu = _conv_in(x, mod[1], norm_mix_g[1:2], bf(conv_w_in[0]))
    return _conv_out(x, gate, cu, mod[1], conv_w[0], bf(conv_w_out[0]), norm_ffn_g[1:2],
                     bf(ffn_w_in[1]), bf(ffn_w_out[1]), final_g.reshape(1, d))
```

```python
import functools

import numpy as np
import jax
import jax.numpy as jnp
from jax import lax
from jax.experimental import pallas as pl
from jax.experimental.pallas import tpu as pltpu

D_MODEL = 1024
HEAD_DIM = 64
N_HEADS = D_MODEL // HEAD_DIM
DILATED_GROUPS = ((128, 1), (512, 4), (2048, 16))
ROPE_THETA = 500000.0
ROPE_DIM = HEAD_DIM // 4
CONV_WIDTH = 3
FFN_HIDDEN = -(-8 * D_MODEL // (3 * 256)) * 256
NORM_EPS = 1e-6
NEG_INF = -1e30

LANES = 128
MXU_WIDTH = 256
HALF_WINDOW = 64
Q_BLOCK = 128
SCORE_SCALE = HEAD_DIM ** -0.5 * float(np.log2(np.e))
VMEM_LIMIT_BYTES = 56 * 1024 * 1024

assert all(w // (2 * d) == HALF_WINDOW for w, d in DILATED_GROUPS)

_F32 = jnp.float32
_BF16 = jnp.bfloat16


def _compiler_params(n_grid_dims):
    return pltpu.CompilerParams(
        dimension_semantics=("arbitrary",) * n_grid_dims,
        vmem_limit_bytes=VMEM_LIMIT_BYTES,
    )


def _resident(block_shape, index_map):
    return pl.BlockSpec(block_shape, index_map, pipeline_mode=pl.Buffered(1))


def _rms_norm(x, g):
    r = lax.rsqrt(jnp.mean(x * x, axis=-1, keepdims=True) + NORM_EPS)
    return (x * r) * g


def _modulated_norm(x, g, shift, scale):
    return _rms_norm(x, g) * (1.0 + scale) + shift


def _modulation_kernel(c_ref, w_ref, b_ref, o_ref):
    c = c_ref[...]
    cond = (c * jax.nn.sigmoid(c)).astype(_BF16)
    o_ref[...] = jnp.dot(cond, w_ref[...].astype(_BF16), preferred_element_type=_F32) + b_ref[...]


def _modulation(c, ada_w, ada_b):
    depth, d, n = ada_w.shape
    b = c.shape[0]
    tn = 1536
    return pl.pallas_call(
        _modulation_kernel,
        grid=(depth, n // tn),
        in_specs=[
            pl.BlockSpec((b, d), lambda i, j: (0, 0)),
            pl.BlockSpec((None, d, tn), lambda i, j: (i, 0, j)),
            pl.BlockSpec((None, 1, tn), lambda i, j: (i, 0, j)),
        ],
        out_specs=pl.BlockSpec((None, b, tn), lambda i, j: (i, 0, j)),
        out_shape=jax.ShapeDtypeStruct((depth, b, n), _F32),
        compiler_params=_compiler_params(2),
        name="modulation",
    )(c, ada_w, ada_b.reshape(depth, 1, n))


def _rope_tables(seq, dilation, q_scale):
    half = ROPE_DIM // 2
    length = seq // dilation
    inv = ROPE_THETA ** (-np.arange(half, dtype=np.float64) * (2.0 / ROPE_DIM))
    pos = (np.arange(length)[None, :] * dilation + np.arange(dilation)[:, None]).astype(np.float64)
    ang = pos[:, :, None] * inv[None, None, :]
    cos, sin = np.cos(ang), np.sin(ang)
    c = np.ones((dilation, length, HEAD_DIM))
    sa = np.zeros((dilation, length, HEAD_DIM))
    sb = np.zeros((dilation, length, HEAD_DIM))
    c[:, :, :half] = cos
    c[:, :, half:ROPE_DIM] = cos
    sa[:, :, :half] = -sin
    sb[:, :, half:ROPE_DIM] = sin
    k_tab = np.stack([np.tile(t, (1, 1, LANES // HEAD_DIM)) for t in (c, sa, sb)])
    return np.stack([k_tab * q_scale, k_tab]).astype(np.float32)


QKV_ROWS = 256


def _qkv_kernel(x_ref, mod_ref, g_ref, w_ref, tab1_ref, tab4_ref, tab16_ref, o1_ref, o4_ref, o16_ref,
                hs1_ref, hs4_ref):
    d, tm = D_MODEL, QKV_ROWS
    d4 = DILATED_GROUPS[1][1]
    n_slabs = d // LANES
    h = _modulated_norm(x_ref[0], g_ref[...], mod_ref[:, 0:d], mod_ref[:, d:2 * d])
    for c in range(n_slabs):
        hs1_ref[c] = h[:, c * LANES:(c + 1) * LANES]
    rows4 = tm // d4
    for c in range(n_slabs):
        hs4_ref[c] = jnp.concatenate([hs1_ref[c, pl.ds(r, rows4, stride=d4), :] for r in range(d4)], axis=0)
    h4 = jnp.concatenate([hs4_ref[c] for c in range(n_slabs)], axis=1)
    rows16 = rows4 // d4
    h16 = jnp.concatenate(
        [jnp.concatenate([hs4_ref[c, pl.ds((r16 % d4) * rows4 + r16 // d4, rows16, stride=d4), :]
                          for c in range(n_slabs)], axis=1)
         for r16 in range(d4 * d4)], axis=0)

    groups = ((h, tab1_ref, o1_ref), (h4, tab4_ref, o4_ref), (h16, tab16_ref, o16_ref))
    for g, (hg, tab_ref, o_ref) in enumerate(groups):
        dilation = DILATED_GROUPS[g][1]
        rows = tm // dilation
        hb = hg.astype(_BF16)
        for part in range(3):
            for j in range(d // MXU_WIDTH):
                col = part * d + j * MXU_WIDTH
                res = jnp.dot(hb, w_ref[:, g * 3 * d + col:g * 3 * d + col + MXU_WIDTH],
                              preferred_element_type=_F32)
                if part < 2:
                    halves = []
                    for s in range(MXU_WIDTH // LANES):
                        xs = res[:, s * LANES:(s + 1) * LANES]
                        tabs = [jnp.concatenate([tab_ref[part, t, r] for r in range(dilation)], axis=0)
                                for t in range(3)]
                        halves.append(xs * tabs[0]
                                      + pltpu.roll(xs, LANES - ROPE_DIM // 2, 1) * tabs[1]
                                      + pltpu.roll(xs, ROPE_DIM // 2, 1) * tabs[2])
                    res = jnp.concatenate(halves, axis=1)
                res = res.astype(_BF16)
                for r in range(dilation):
                    o_ref[0, r, :, col:col + MXU_WIDTH] = res[r * rows:(r + 1) * rows]


def _qkv(x, mod, g, w):
    b, s, d = x.shape
    tm = QKV_ROWS
    dils = [dil for _, dil in DILATED_GROUPS]
    assert dils == [1, 4, 16]
    tabs = [jnp.asarray(_rope_tables(s, dil, SCORE_SCALE)) for dil in dils]
    return pl.pallas_call(
        _qkv_kernel,
        grid=(b, s // tm),
        in_specs=[
            pl.BlockSpec((1, tm, d), lambda i, j: (i, j, 0)),
            pl.BlockSpec((None, 1, 6 * d), lambda i, j: (i, 0, 0)),
            pl.BlockSpec((1, d), lambda i, j: (0, 0)),
            _resident(w.shape, lambda i, j: (0, 0)),
        ] + [pl.BlockSpec((2, 3, dil, tm // dil, LANES), lambda i, j: (0, 0, 0, j, 0)) for dil in dils],
        out_specs=[pl.BlockSpec((1, dil, tm // dil, 3 * d), lambda i, j: (i, 0, j, 0)) for dil in dils],
        out_shape=[jax.ShapeDtypeStruct((b, dil, s // dil, 3 * d), _BF16) for dil in dils],
        scratch_shapes=[pltpu.VMEM((d // LANES, tm, LANES), _F32)] * 2,
        compiler_params=_compiler_params(2),
        name="qkv",
    )(x, mod, g, w, *tabs)


def _band_bias(n_keys):
    i = np.arange(Q_BLOCK)[:, None]
    j = np.arange(n_keys)[None, :]
    one = np.stack([np.where(np.abs(off + i - j) <= HALF_WINDOW, 0.0, NEG_INF)
                    for off in (0, HALF_WINDOW, 2 * HALF_WINDOW)])
    return np.concatenate([one, one], axis=1).astype(np.float32)


def _pair_bias():
    band = _band_bias(Q_BLOCK)[0]
    masked = np.full_like(band, NEG_INF)
    return np.stack([np.concatenate([band, masked], axis=1), np.concatenate([masked, band], axis=1)])


ATTN_UNROLL = 16


def _attention_kernel(q0, k0, v0, q1, k1, v1, q2, k2, v2, bias_w_ref, bias_p_ref, o_ref,
                      s4_ref, s1_ref, *, seq):
    lane = lax.broadcasted_iota(jnp.int32, (Q_BLOCK, LANES), 1)
    head_a = lane < HEAD_DIM
    d4 = DILATED_GROUPS[1][1]
    len4 = seq // d4

    def block_stats(q, k, v, bias):
        n_keys = k.shape[0]
        zero = jnp.zeros_like(q)
        q2 = jnp.concatenate([jnp.where(head_a, q, zero), jnp.where(head_a, zero, q)], axis=0)
        s = lax.dot_general(q2, k, (((1,), (1,)), ((), ())), preferred_element_type=_F32) + bias
        m = jnp.max(s, axis=-1, keepdims=True)
        p = jnp.exp2(s - m).astype(_BF16)
        v_ext = jnp.concatenate([v, jnp.ones((n_keys, LANES), _BF16)], axis=1)
        pv = jnp.dot(p, v_ext, preferred_element_type=_F32)
        m_pair = jnp.where(head_a, jnp.broadcast_to(m[:Q_BLOCK], (Q_BLOCK, LANES)),
                           jnp.broadcast_to(m[Q_BLOCK:], (Q_BLOCK, LANES)))
        l_pair = jnp.where(head_a, pv[:Q_BLOCK, LANES:], pv[Q_BLOCK:, LANES:])
        acc_pair = jnp.where(head_a, pv[:Q_BLOCK, :LANES], pv[Q_BLOCK:, :LANES])
        return m_pair, l_pair, acc_pair

    def merge(old, new):
        m = jnp.maximum(old[0], new[0])
        a_old = jnp.exp2(old[0] - m)
        a_new = jnp.exp2(new[0] - m)
        return m, a_old * old[1] + a_new * new[1], a_old * old[2] + a_new * new[2]

    def load_block(q_ref, k_ref, v_ref, r, j, length):
        n_keys = min(length, Q_BLOCK + 2 * HALF_WINDOW)
        m0 = pl.multiple_of(j * Q_BLOCK, Q_BLOCK)
        k_start = pl.multiple_of(jnp.clip(m0 - HALF_WINDOW, 0, length - n_keys), HALF_WINDOW)
        bias = bias_w_ref[(m0 - k_start) // HALF_WINDOW]
        return (q_ref[0, r, pl.ds(m0, Q_BLOCK), :], k_ref[0, r, pl.ds(k_start, n_keys), :],
                v_ref[0, r, pl.ds(k_start, n_keys), :], bias)

    def load_pair_block(q_ref, k_ref, v_ref, r):
        first = pl.multiple_of((r // 2) * 2, 2)
        keys = lambda ref: ref[0, pl.ds(first, 2), :, :].reshape(2 * Q_BLOCK, LANES)
        return q_ref[0, r], keys(k_ref), keys(v_ref), bias_p_ref[r % 2]

    def group16(idx, carry):
        a, c = idx // d4, idx % d4
        state = block_stats(*load_pair_block(q2, k2, v2, idx))
        rows = pl.ds(c * len4 + a, Q_BLOCK, stride=d4)
        for t in range(3):
            s4_ref[t, rows, :] = state[t]
        return carry

    def group4(idx, carry):
        c, j = idx // (len4 // Q_BLOCK), idx % (len4 // Q_BLOCK)
        new = block_stats(*load_block(q1, k1, v1, c, j, len4))
        src = pl.ds(pl.multiple_of(c * len4 + j * Q_BLOCK, Q_BLOCK), Q_BLOCK)
        state = merge([s4_ref[t, src, :] for t in range(3)], new)
        rows = pl.ds(j * Q_BLOCK * d4 + c, Q_BLOCK, stride=d4)
        for t in range(3):
            s1_ref[t, rows, :] = state[t]
        return carry

    def group1(j, carry):
        new = block_stats(*load_block(q0, k0, v0, 0, j, seq))
        rows = pl.ds(pl.multiple_of(j * Q_BLOCK, Q_BLOCK), Q_BLOCK)
        _, l, acc = merge([s1_ref[t, rows, :] for t in range(3)], new)
        o_ref[0, rows, :] = (acc / l).astype(o_ref.dtype)
        return carry

    n_blocks = seq // Q_BLOCK
    lax.fori_loop(0, n_blocks, group16, 0, unroll=ATTN_UNROLL)
    lax.fori_loop(0, n_blocks, group4, 0, unroll=ATTN_UNROLL)
    lax.fori_loop(0, n_blocks, group1, 0, unroll=ATTN_UNROLL)


def _attention(qkv_groups, seq):
    b = qkv_groups[0].shape[0]
    d = D_MODEL
    n_pairs = d // LANES
    in_specs, args = [], []
    for qkv, (_, dil) in zip(qkv_groups, DILATED_GROUPS):
        for part in range(3):
            in_specs.append(pl.BlockSpec((1, dil, seq // dil, LANES),
                                         lambda i, p, part=part: (i, 0, 0, part * n_pairs + p)))
            args.append(qkv)
    wide = Q_BLOCK + 2 * HALF_WINDOW
    assert seq // DILATED_GROUPS[2][1] == Q_BLOCK and seq // DILATED_GROUPS[1][1] >= wide
    in_specs += [pl.BlockSpec((3, 2 * Q_BLOCK, wide), lambda i, p: (0, 0, 0)),
                 pl.BlockSpec((2, 2 * Q_BLOCK, wide), lambda i, p: (0, 0, 0))]
    args += [jnp.asarray(_band_bias(wide)), jnp.asarray(_pair_bias())]
    return pl.pallas_call(
        functools.partial(_attention_kernel, seq=seq),
        grid=(b, n_pairs),
        in_specs=in_specs,
        out_specs=pl.BlockSpec((1, seq, LANES), lambda i, p: (i, 0, p)),
        out_shape=jax.ShapeDtypeStruct((b, seq, d), _BF16),
        scratch_shapes=[pltpu.VMEM((3, seq, LANES), _F32),
                        pltpu.VMEM((3, seq, LANES), _F32)],
        compiler_params=_compiler_params(2),
        name="dilated_attention",
    )(*args)


FFN_CHUNK = MXU_WIDTH


TAIL_ROWS = 512
TAIL_SUB_ROWS = 256


def _ffn_block(x1, mod_ref, g_ref, w_in_ref, w_out_ref, act_ref, rows):
    d, f = D_MODEL, FFN_HIDDEN
    h = _modulated_norm(x1, g_ref[...], mod_ref[:, 3 * d:4 * d], mod_ref[:, 4 * d:5 * d]).astype(_BF16)
    for c in range(f // FFN_CHUNK):
        lo = c * FFN_CHUNK
        gate = jnp.dot(h, w_in_ref[:, lo:lo + FFN_CHUNK], preferred_element_type=_F32)
        up = jnp.dot(h, w_in_ref[:, f + lo:f + lo + FFN_CHUNK], preferred_element_type=_F32)
        act_ref[rows, lo:lo + FFN_CHUNK] = (gate * jax.nn.sigmoid(gate) * up).astype(_BF16)
    y = jnp.dot(act_ref[rows, :], w_out_ref[...], preferred_element_type=_F32)
    return x1 + mod_ref[:, 5 * d:6 * d] * y


def _post_attention_kernel(x_ref, o_ref, mod_ref, w_o_ref, g_ref, w_in_ref, w_out_ref, out_ref, act_ref):
    d = D_MODEL
    for sub in range(TAIL_ROWS // TAIL_SUB_ROWS):
        rows = slice(sub * TAIL_SUB_ROWS, (sub + 1) * TAIL_SUB_ROWS)
        y = jnp.dot(o_ref[0, rows, :], w_o_ref[...], preferred_element_type=_F32)
        x1 = x_ref[0, rows, :] + mod_ref[:, 2 * d:3 * d] * y
        out_ref[0, rows, :] = _ffn_block(x1, mod_ref, g_ref, w_in_ref, w_out_ref, act_ref, rows)


def _post_attention(x, o, mod, w_o, g, w_in, w_out):
    b, s, d = x.shape
    f = FFN_HIDDEN
    tm = TAIL_ROWS
    return pl.pallas_call(
        _post_attention_kernel,
        grid=(b, s // tm),
        in_specs=[
            pl.BlockSpec((1, tm, d), lambda i, j: (i, j, 0)),
            pl.BlockSpec((1, tm, d), lambda i, j: (i, j, 0)),
            pl.BlockSpec((None, 1, 6 * d), lambda i, j: (i, 0, 0)),
            _resident((d, d), lambda i, j: (0, 0)),
            pl.BlockSpec((1, d), lambda i, j: (0, 0)),
            _resident((d, 2 * f), lambda i, j: (0, 0)),
            _resident((f, d), lambda i, j: (0, 0)),
        ],
        out_specs=pl.BlockSpec((1, tm, d), lambda i, j: (i, j, 0)),
        out_shape=jax.ShapeDtypeStruct((b, s, d), _F32),
        scratch_shapes=[pltpu.VMEM((tm, f), _BF16)],
        compiler_params=_compiler_params(2),
        name="post_attention",
    )(x, o, mod, w_o, g, w_in, w_out)


def _conv_in_kernel(x_ref, mod_ref, g_ref, w_ref, gate_ref, cu_ref):
    d = D_MODEL
    h = _modulated_norm(x_ref[0], g_ref[...], mod_ref[:, 0:d], mod_ref[:, d:2 * d]).astype(_BF16)
    for j in range(d // MXU_WIDTH):
        lo = j * MXU_WIDTH
        gate_ref[0, :, lo:lo + MXU_WIDTH] = jnp.dot(
            h, w_ref[:, lo:lo + MXU_WIDTH], preferred_element_type=_F32).astype(_BF16)
        c_gate = jnp.dot(h, w_ref[:, d + lo:d + lo + MXU_WIDTH], preferred_element_type=_F32)
        u = jnp.dot(h, w_ref[:, 2 * d + lo:2 * d + lo + MXU_WIDTH], preferred_element_type=_F32)
        cu_ref[0, :, lo:lo + MXU_WIDTH] = (c_gate * u).astype(_BF16)


def _conv_in(x, mod, g, w):
    b, s, d = x.shape
    tm = 512
    tile = pl.BlockSpec((1, tm, d), lambda i, j: (i, j, 0))
    return pl.pallas_call(
        _conv_in_kernel,
        grid=(b, s // tm),
        in_specs=[
            tile,
            pl.BlockSpec((None, 1, 6 * d), lambda i, j: (i, 0, 0)),
            pl.BlockSpec((1, d), lambda i, j: (0, 0)),
            _resident((d, 3 * d), lambda i, j: (0, 0)),
        ],
        out_specs=[tile, tile],
        out_shape=[jax.ShapeDtypeStruct((b, s, d), _BF16)] * 2,
        compiler_params=_compiler_params(2),
        name="conv_in",
    )(x, mod, g, w)


HALO_ROWS = 16


def _conv_out_kernel(x_ref, gate_ref, cu_ref, prev_ref, next_ref, mod_ref, cw_ref, w_o_ref, g_ref,
                     w_in_ref, w_out_ref, gf_ref, out_ref, act_ref, *, tm):
    d = D_MODEL
    j = pl.program_id(1)
    cu = cu_ref[0].astype(_F32)
    row = lax.broadcasted_iota(jnp.int32, (tm, d), 0)
    prev_row = jnp.where(j > 0, prev_ref[0, HALO_ROWS - 1:HALO_ROWS, :].astype(_F32), 0.0)
    next_row = jnp.where(j < pl.num_programs(1) - 1, next_ref[0, 0:1, :].astype(_F32), 0.0)
    before = jnp.where(row == 0, prev_row, pltpu.roll(cu, 1, 0))
    after = jnp.where(row == tm - 1, next_row, pltpu.roll(cu, tm - 1, 0))
    z = before * cw_ref[0:1, :] + cu * cw_ref[1:2, :] + after * cw_ref[2:3, :]
    gated = (gate_ref[0].astype(_F32) * z).astype(_BF16)
    for sub in range(tm // TAIL_SUB_ROWS):
        rows = slice(sub * TAIL_SUB_ROWS, (sub + 1) * TAIL_SUB_ROWS)
        y = jnp.dot(gated[rows], w_o_ref[...], preferred_element_type=_F32)
        x1 = x_ref[0, rows, :] + mod_ref[:, 2 * d:3 * d] * y
        x2 = _ffn_block(x1, mod_ref, g_ref, w_in_ref, w_out_ref, act_ref, rows)
        out_ref[0, rows, :] = _rms_norm(x2, gf_ref[...])


def _conv_out(x, gate, cu, mod, conv_w, w_o, g, w_in, w_out, final_g):
    b, s, d = x.shape
    f = FFN_HIDDEN
    tm = TAIL_ROWS
    per_tile = tm // HALO_ROWS
    n_halo = s // HALO_ROWS
    tile = pl.BlockSpec((1, tm, d), lambda i, j: (i, j, 0))
    return pl.pallas_call(
        functools.partial(_conv_out_kernel, tm=tm),
        grid=(b, s // tm),
        in_specs=[
            tile, tile, tile,
            pl.BlockSpec((1, HALO_ROWS, d), lambda i, j: (i, jnp.maximum(j * per_tile - 1, 0), 0)),
            pl.BlockSpec((1, HALO_ROWS, d), lambda i, j: (i, jnp.minimum((j + 1) * per_tile, n_halo - 1), 0)),
            pl.BlockSpec((None, 1, 6 * d), lambda i, j: (i, 0, 0)),
            pl.BlockSpec((CONV_WIDTH, d), lambda i, j: (0, 0)),
            _resident((d, d), lambda i, j: (0, 0)),
            pl.BlockSpec((1, d), lambda i, j: (0, 0)),
            _resident((d, 2 * f), lambda i, j: (0, 0)),
            _resident((f, d), lambda i, j: (0, 0)),
            pl.BlockSpec((1, d), lambda i, j: (0, 0)),
        ],
        out_specs=tile,
        out_shape=jax.ShapeDtypeStruct((b, s, d), _F32),
        scratch_shapes=[pltpu.VMEM((tm, f), _BF16)],
        compiler_params=_compiler_params(2),
        name="conv_out",
    )(x, gate, cu, cu, cu, mod, conv_w, w_o, g, w_in, w_out, final_g)


def kernel(x, c, attn_w_qkv, attn_w_o, conv_w_in, conv_w, conv_w_out, ada_w, ada_b,
           norm_mix_g, norm_ffn_g, ffn_w_in, ffn_w_out, final_g):
    b, s, d = x.shape
    assert d == D_MODEL and attn_w_qkv.shape[0] == 1 and conv_w_in.shape[0] == 1 and ada_w.shape[0] == 2
    mod = _modulation(c, ada_w, ada_b).reshape(2, b, 1, 6 * d)
    bf = lambda w: w.astype(_BF16)

    qkv = _qkv(x, mod[0], norm_mix_g[0:1], bf(attn_w_qkv[0]))
    o = _attention(qkv, s)
    x = _post_attention(x, o, mod[0], bf(attn_w_o[0]), norm_ffn_g[0:1], bf(ffn_w_in[0]), bf(ffn_w_out[0]))

    gate, cu = _conv_in(x, mod[1], norm_mix_g[1:2], bf(conv_w_in[0]))
    return _conv_out(x, gate, cu, mod[1], conv_w[0], bf(conv_w_out[0]), norm_ffn_g[1:2],
                     bf(ffn_w_in[1]), bf(ffn_w_out[1]), final_g.reshape(1, d))
```

```python
import functools

import numpy as np
import jax
import jax.numpy as jnp
from jax import lax
from jax.experimental import pallas as pl
from jax.experimental.pallas import tpu as pltpu

D_MODEL = 1024
HEAD_DIM = 64
N_HEADS = D_MODEL // HEAD_DIM
DILATED_GROUPS = ((128, 1), (512, 4), (2048, 16))
ROPE_THETA = 500000.0
ROPE_DIM = HEAD_DIM // 4
CONV_WIDTH = 3
FFN_HIDDEN = -(-8 * D_MODEL // (3 * 256)) * 256
NORM_EPS = 1e-6
NEG_INF = -1e30

LANES = 128
MXU_WIDTH = 256
HALF_WINDOW = 64
Q_BLOCK = 128
SCORE_SCALE = HEAD_DIM ** -0.5 * float(np.log2(np.e))
VMEM_LIMIT_BYTES = 56 * 1024 * 1024

assert all(w // (2 * d) == HALF_WINDOW for w, d in DILATED_GROUPS)

_F32 = jnp.float32
_BF16 = jnp.bfloat16


def _compiler_params(n_grid_dims):
    return pltpu.CompilerParams(
        dimension_semantics=("arbitrary",) * n_grid_dims,
        vmem_limit_bytes=VMEM_LIMIT_BYTES,
    )


def _resident(block_shape, index_map):
    return pl.BlockSpec(block_shape, index_map, pipeline_mode=pl.Buffered(1))


def _rms_norm(x, g):
    r = lax.rsqrt(jnp.mean(x * x, axis=-1, keepdims=True) + NORM_EPS)
    return (x * r) * g


def _modulated_norm(x, g, shift, scale):
    return _rms_norm(x, g) * (1.0 + scale) + shift


def _modulation_kernel(c_ref, w_ref, b_ref, o_ref):
    c = c_ref[...]
    cond = (c * jax.nn.sigmoid(c)).astype(_BF16)
    o_ref[...] = jnp.dot(cond, w_ref[...].astype(_BF16), preferred_element_type=_F32) + b_ref[...]


def _modulation(c, ada_w, ada_b):
    depth, d, n = ada_w.shape
    b = c.shape[0]
    tn = 1536
    return pl.pallas_call(
        _modulation_kernel,
        grid=(depth, n // tn),
        in_specs=[
            pl.BlockSpec((b, d), lambda i, j: (0, 0)),
            pl.BlockSpec((None, d, tn), lambda i, j: (i, 0, j)),
            pl.BlockSpec((None, 1, tn), lambda i, j: (i, 0, j)),
        ],
        out_specs=pl.BlockSpec((None, b, tn), lambda i, j: (i, 0, j)),
        out_shape=jax.ShapeDtypeStruct((depth, b, n), _F32),
        compiler_params=_compiler_params(2),
        name="modulation",
    )(c, ada_w, ada_b.reshape(depth, 1, n))


CAST_ROWS = 16


def _cast_specs(weights, n_inner, total_steps):
    in_specs, out_specs, out_shapes, args, pers = [], [], [], [], []
    for w, layer in weights:
        rows, cols = w.shape[1:]
        n_blocks = max(nb for nb in range(1, total_steps + 1)
                       if total_steps % nb == 0 and rows % (nb * CAST_ROWS) == 0)
        block_rows = rows // n_blocks
        block = lambda i, j, per=total_steps // n_blocks: (i * n_inner + j) // per
        in_specs.append(pl.BlockSpec((None, block_rows, cols),
                                     lambda i, j, layer=layer, block=block: (layer, block(i, j), 0)))
        out_specs.append(pl.BlockSpec((block_rows, cols), lambda i, j, block=block: (block(i, j), 0)))
        out_shapes.append(jax.ShapeDtypeStruct((rows, cols), _BF16))
        args.append(w)
        pers.append(total_steps // n_blocks)
    return in_specs, out_specs, out_shapes, args, tuple(pers)


CAST_GUARD_STEPS = 4


def _cast_blocks(in_refs, out_refs, pers, step, guarded):
    for w_ref, o_ref, per in zip(in_refs, out_refs, pers):
        if guarded and per >= CAST_GUARD_STEPS:
            @pl.when(step % per == 0)
            def _(w_ref=w_ref, o_ref=o_ref):
                o_ref[...] = w_ref[...].astype(_BF16)
        elif not guarded and per < CAST_GUARD_STEPS:
            o_ref[...] = w_ref[...].astype(_BF16)


def _rope_tables(seq, dilation, q_scale):
    half = ROPE_DIM // 2
    length = seq // dilation
    inv = ROPE_THETA ** (-np.arange(half, dtype=np.float64) * (2.0 / ROPE_DIM))
    pos = (np.arange(length)[None, :] * dilation + np.arange(dilation)[:, None]).astype(np.float64)
    ang = pos[:, :, None] * inv[None, None, :]
    cos, sin = np.cos(ang), np.sin(ang)
    c = np.ones((dilation, length, HEAD_DIM))
    sa = np.zeros((dilation, length, HEAD_DIM))
    sb = np.zeros((dilation, length, HEAD_DIM))
    c[:, :, :half] = cos
    c[:, :, half:ROPE_DIM] = cos
    sa[:, :, :half] = -sin
    sb[:, :, half:ROPE_DIM] = sin
    k_tab = np.stack([np.tile(t, (1, 1, LANES // HEAD_DIM)) for t in (c, sa, sb)])
    return np.stack([k_tab * q_scale, k_tab]).astype(np.float32)


QKV_ROWS = 256


def _qkv_kernel(x_ref, mod_ref, g_ref, w_ref, tab1_ref, tab4_ref, tab16_ref, *refs, cast_pers):
    n_cast = len(cast_pers)
    cast_in, (o1_ref, o4_ref, o16_ref) = refs[:n_cast], refs[n_cast:n_cast + 3]
    cast_out, (hs1_ref, hs4_ref) = refs[n_cast + 3:2 * n_cast + 3], refs[2 * n_cast + 3:]
    step = pl.program_id(0) * pl.num_programs(1) + pl.program_id(1)
    _cast_blocks(cast_in, cast_out, cast_pers, step, True)
    d, tm = D_MODEL, QKV_ROWS
    d4 = DILATED_GROUPS[1][1]
    n_slabs = d // LANES
    h = _modulated_norm(x_ref[0], g_ref[...], mod_ref[:, 0:d], mod_ref[:, d:2 * d])
    for c in range(n_slabs):
        hs1_ref[c] = h[:, c * LANES:(c + 1) * LANES]
    rows4 = tm // d4
    for c in range(n_slabs):
        hs4_ref[c] = jnp.concatenate([hs1_ref[c, pl.ds(r, rows4, stride=d4), :] for r in range(d4)], axis=0)
    h4 = jnp.concatenate([hs4_ref[c] for c in range(n_slabs)], axis=1)
    rows16 = rows4 // d4
    h16 = jnp.concatenate(
        [jnp.concatenate([hs4_ref[c, pl.ds((r16 % d4) * rows4 + r16 // d4, rows16, stride=d4), :]
                          for c in range(n_slabs)], axis=1)
         for r16 in range(d4 * d4)], axis=0)

    groups = ((h, tab1_ref, o1_ref), (h4, tab4_ref, o4_ref), (h16, tab16_ref, o16_ref))
    for g, (hg, tab_ref, o_ref) in enumerate(groups):
        dilation = DILATED_GROUPS[g][1]
        rows = tm // dilation
        hb = hg.astype(_BF16)
        if g == 1:
            _cast_blocks(cast_in, cast_out, cast_pers, step, False)
        for part in range(3):
            for j in range(d // MXU_WIDTH):
                col = part * d + j * MXU_WIDTH
                res = jnp.dot(hb, w_ref[:, g * 3 * d + col:g * 3 * d + col + MXU_WIDTH],
                              preferred_element_type=_F32)
                if part < 2:
                    halves = []
                    for s in range(MXU_WIDTH // LANES):
                        xs = res[:, s * LANES:(s + 1) * LANES]
                        tabs = [jnp.concatenate([tab_ref[part, t, r] for r in range(dilation)], axis=0)
                                for t in range(3)]
                        halves.append(xs * tabs[0]
                                      + pltpu.roll(xs, LANES - ROPE_DIM // 2, 1) * tabs[1]
                                      + pltpu.roll(xs, ROPE_DIM // 2, 1) * tabs[2])
                    res = jnp.concatenate(halves, axis=1)
                res = res.astype(_BF16)
                for r in range(dilation):
                    o_ref[0, r, :, col:col + MXU_WIDTH] = res[r * rows:(r + 1) * rows]


def _qkv(x, mod, g, w, cast_weights):
    b, s, d = x.shape
    tm = QKV_ROWS
    dils = [dil for _, dil in DILATED_GROUPS]
    assert dils == [1, 4, 16]
    tabs = [jnp.asarray(_rope_tables(s, dil, SCORE_SCALE)) for dil in dils]
    c_in, c_out, c_shapes, c_args, c_pers = _cast_specs(cast_weights, s // tm, b * (s // tm))
    return pl.pallas_call(
        functools.partial(_qkv_kernel, cast_pers=c_pers),
        grid=(b, s // tm),
        in_specs=[
            pl.BlockSpec((1, tm, d), lambda i, j: (i, j, 0)),
            pl.BlockSpec((None, 1, 6 * d), lambda i, j: (i, 0, 0)),
            pl.BlockSpec((1, d), lambda i, j: (0, 0)),
            _resident(w.shape, lambda i, j: (0, 0)),
        ] + [pl.BlockSpec((2, 3, dil, tm // dil, LANES), lambda i, j: (0, 0, 0, j, 0)) for dil in dils] + c_in,
        out_specs=[pl.BlockSpec((1, dil, tm // dil, 3 * d), lambda i, j: (i, 0, j, 0)) for dil in dils] + c_out,
        out_shape=[jax.ShapeDtypeStruct((b, dil, s // dil, 3 * d), _BF16) for dil in dils] + c_shapes,
        scratch_shapes=[pltpu.VMEM((d // LANES, tm, LANES), _F32)] * 2,
        compiler_params=_compiler_params(2),
        name="qkv",
    )(x, mod, g, w, *tabs, *c_args)


def _band_bias(n_keys):
    i = np.arange(Q_BLOCK)[:, None]
    j = np.arange(n_keys)[None, :]
    one = np.stack([np.where(np.abs(off + i - j) <= HALF_WINDOW, 0.0, NEG_INF)
                    for off in (0, HALF_WINDOW, 2 * HALF_WINDOW)])
    return np.concatenate([one, one], axis=1).astype(np.float32)


def _pair_bias():
    band = _band_bias(Q_BLOCK)[0]
    masked = np.full_like(band, NEG_INF)
    return np.stack([np.concatenate([band, masked], axis=1), np.concatenate([masked, band], axis=1)])


ATTN_UNROLL = 16
ATTN_PAIRS = 2


def _attention_kernel(q0, k0, v0, q1, k1, v1, q2, k2, v2, bias_w_ref, bias_p_ref, *refs, seq, cast_pers):
    n_cast = len(cast_pers)
    cast_in, o_ref = refs[:n_cast], refs[n_cast]
    cast_out, (s4_ref, s1_ref) = refs[n_cast + 1:2 * n_cast + 1], refs[2 * n_cast + 1:]
    step = pl.program_id(0) * pl.num_programs(1) + pl.program_id(1)
    _cast_blocks(cast_in, cast_out, cast_pers, step, True)
    lane = lax.broadcasted_iota(jnp.int32, (Q_BLOCK, LANES), 1)
    head_a = lane < HEAD_DIM
    for pair in range(ATTN_PAIRS):
        if pair == 1:
            _cast_blocks(cast_in, cast_out, cast_pers, step, False)
        _attention_pair(q0, k0, v0, q1, k1, v1, q2, k2, v2, bias_w_ref, bias_p_ref, o_ref,
                        s4_ref.at[pair], s1_ref.at[pair], pl.ds(pair * LANES, LANES), head_a, seq)


def _attention_pair(q0, k0, v0, q1, k1, v1, q2, k2, v2, bias_w_ref, bias_p_ref, o_ref,
                    s4_ref, s1_ref, lanes, head_a, seq):
    d4 = DILATED_GROUPS[1][1]
    len4 = seq // d4

    def block_stats(q, k, v, bias):
        n_keys = k.shape[0]
        zero = jnp.zeros_like(q)
        q2 = jnp.concatenate([jnp.where(head_a, q, zero), jnp.where(head_a, zero, q)], axis=0)
        s = lax.dot_general(q2, k, (((1,), (1,)), ((), ())), preferred_element_type=_F32) + bias
        m = jnp.max(s, axis=-1, keepdims=True)
        p = jnp.exp2(s - m).astype(_BF16)
        v_ext = jnp.concatenate([v, jnp.ones((n_keys, LANES), _BF16)], axis=1)
        pv = jnp.dot(p, v_ext, preferred_element_type=_F32)
        m_pair = jnp.where(head_a, jnp.broadcast_to(m[:Q_BLOCK], (Q_BLOCK, LANES)),
                           jnp.broadcast_to(m[Q_BLOCK:], (Q_BLOCK, LANES)))
        l_pair = jnp.where(head_a, pv[:Q_BLOCK, LANES:], pv[Q_BLOCK:, LANES:])
        acc_pair = jnp.where(head_a, pv[:Q_BLOCK, :LANES], pv[Q_BLOCK:, :LANES])
        return m_pair, l_pair, acc_pair

    def merge(old, new):
        m = jnp.maximum(old[0], new[0])
        a_old = jnp.exp2(old[0] - m)
        a_new = jnp.exp2(new[0] - m)
        return m, a_old * old[1] + a_new * new[1], a_old * old[2] + a_new * new[2]

    def load_block(q_ref, k_ref, v_ref, r, j, length):
        n_keys = min(length, Q_BLOCK + 2 * HALF_WINDOW)
        m0 = pl.multiple_of(j * Q_BLOCK, Q_BLOCK)
        k_start = pl.multiple_of(jnp.clip(m0 - HALF_WINDOW, 0, length - n_keys), HALF_WINDOW)
        bias = bias_w_ref[(m0 - k_start) // HALF_WINDOW]
        return (q_ref[0, r, pl.ds(m0, Q_BLOCK), lanes], k_ref[0, r, pl.ds(k_start, n_keys), lanes],
                v_ref[0, r, pl.ds(k_start, n_keys), lanes], bias)

    def load_pair_block(q_ref, k_ref, v_ref, r):
        first = pl.multiple_of((r // 2) * 2, 2)
        keys = lambda ref: ref[0, pl.ds(first, 2), :, lanes].reshape(2 * Q_BLOCK, LANES)
        return q_ref[0, r, :, lanes], keys(k_ref), keys(v_ref), bias_p_ref[r % 2]

    def group16(idx, carry):
        a, c = idx // d4, idx % d4
        state = block_stats(*load_pair_block(q2, k2, v2, idx))
        rows = pl.ds(c * len4 + a, Q_BLOCK, stride=d4)
        for t in range(3):
            s4_ref[t, rows, :] = state[t]
        return carry

    def group4(idx, carry):
        c, j = idx // (len4 // Q_BLOCK), idx % (len4 // Q_BLOCK)
        new = block_stats(*load_block(q1, k1, v1, c, j, len4))
        src = pl.ds(pl.multiple_of(c * len4 + j * Q_BLOCK, Q_BLOCK), Q_BLOCK)
        state = merge([s4_ref[t, src, :] for t in range(3)], new)
        rows = pl.ds(j * Q_BLOCK * d4 + c, Q_BLOCK, stride=d4)
        for t in range(3):
            s1_ref[t, rows, :] = state[t]
        return carry

    def group1(j, carry):
        new = block_stats(*load_block(q0, k0, v0, 0, j, seq))
        rows = pl.ds(pl.multiple_of(j * Q_BLOCK, Q_BLOCK), Q_BLOCK)
        _, l, acc = merge([s1_ref[t, rows, :] for t in range(3)], new)
        o_ref[0, rows, lanes] = (acc / l).astype(o_ref.dtype)
        return carry

    n_blocks = seq // Q_BLOCK
    lax.fori_loop(0, n_blocks, group16, 0, unroll=ATTN_UNROLL)
    lax.fori_loop(0, n_blocks, group4, 0, unroll=ATTN_UNROLL)
    lax.fori_loop(0, n_blocks, group1, 0, unroll=ATTN_UNROLL)


def _attention(qkv_groups, seq, cast_weights):
    b = qkv_groups[0].shape[0]
    d = D_MODEL
    width = ATTN_PAIRS * LANES
    n_steps = d // width
    in_specs, args = [], []
    for qkv, (_, dil) in zip(qkv_groups, DILATED_GROUPS):
        for part in range(3):
            in_specs.append(pl.BlockSpec((1, dil, seq // dil, width),
                                         lambda i, p, part=part: (i, 0, 0, part * n_steps + p)))
            args.append(qkv)
    wide = Q_BLOCK + 2 * HALF_WINDOW
    assert seq // DILATED_GROUPS[2][1] == Q_BLOCK and seq // DILATED_GROUPS[1][1] >= wide
    in_specs += [pl.BlockSpec((3, 2 * Q_BLOCK, wide), lambda i, p: (0, 0, 0)),
                 pl.BlockSpec((2, 2 * Q_BLOCK, wide), lambda i, p: (0, 0, 0))]
    args += [jnp.asarray(_band_bias(wide)), jnp.asarray(_pair_bias())]
    c_in, c_out, c_shapes, c_args, c_pers = _cast_specs(cast_weights, n_steps, b * n_steps)
    return pl.pallas_call(
        functools.partial(_attention_kernel, seq=seq, cast_pers=c_pers),
        grid=(b, n_steps),
        in_specs=in_specs + c_in,
        out_specs=[pl.BlockSpec((1, seq, width), lambda i, p: (i, 0, p))] + c_out,
        out_shape=[jax.ShapeDtypeStruct((b, seq, d), _BF16)] + c_shapes,
        scratch_shapes=[pltpu.VMEM((ATTN_PAIRS, 3, seq, LANES), _F32),
                        pltpu.VMEM((ATTN_PAIRS, 3, seq, LANES), _F32)],
        compiler_params=_compiler_params(2),
        name="dilated_attention",
    )(*args, *c_args)


FFN_CHUNK = MXU_WIDTH


TAIL_ROWS = 512
TAIL_SUB_ROWS = 256


def _ffn_block(x1, mod_ref, g_ref, w_in_ref, w_out_ref, act_ref, rows):
    d, f = D_MODEL, FFN_HIDDEN
    h = _modulated_norm(x1, g_ref[...], mod_ref[:, 3 * d:4 * d], mod_ref[:, 4 * d:5 * d]).astype(_BF16)
    for c in range(f // FFN_CHUNK):
        lo = c * FFN_CHUNK
        gate = jnp.dot(h, w_in_ref[:, lo:lo + FFN_CHUNK], preferred_element_type=_F32)
        up = jnp.dot(h, w_in_ref[:, f + lo:f + lo + FFN_CHUNK], preferred_element_type=_F32)
        act_ref[rows, lo:lo + FFN_CHUNK] = (gate * jax.nn.sigmoid(gate) * up).astype(_BF16)
    y = jnp.dot(act_ref[rows, :], w_out_ref[...], preferred_element_type=_F32)
    return x1 + mod_ref[:, 5 * d:6 * d] * y


def _post_attention_kernel(x_ref, o_ref, mod_ref, w_o_ref, g_ref, w_in_ref, w_out_ref, out_ref, act_ref):
    d = D_MODEL
    for sub in range(TAIL_ROWS // TAIL_SUB_ROWS):
        rows = slice(sub * TAIL_SUB_ROWS, (sub + 1) * TAIL_SUB_ROWS)
        y = jnp.dot(o_ref[0, rows, :], w_o_ref[...], preferred_element_type=_F32)
        x1 = x_ref[0, rows, :] + mod_ref[:, 2 * d:3 * d] * y
        out_ref[0, rows, :] = _ffn_block(x1, mod_ref, g_ref, w_in_ref, w_out_ref, act_ref, rows)


def _post_attention(x, o, mod, w_o, g, w_in, w_out):
    b, s, d = x.shape
    f = FFN_HIDDEN
    tm = TAIL_ROWS
    return pl.pallas_call(
        _post_attention_kernel,
        grid=(b, s // tm),
        in_specs=[
            pl.BlockSpec((1, tm, d), lambda i, j: (i, j, 0)),
            pl.BlockSpec((1, tm, d), lambda i, j: (i, j, 0)),
            pl.BlockSpec((None, 1, 6 * d), lambda i, j: (i, 0, 0)),
            _resident((d, d), lambda i, j: (0, 0)),
            pl.BlockSpec((1, d), lambda i, j: (0, 0)),
            _resident((d, 2 * f), lambda i, j: (0, 0)),
            _resident((f, d), lambda i, j: (0, 0)),
        ],
        out_specs=pl.BlockSpec((1, tm, d), lambda i, j: (i, j, 0)),
        out_shape=jax.ShapeDtypeStruct((b, s, d), _F32),
        scratch_shapes=[pltpu.VMEM((tm, f), _BF16)],
        compiler_params=_compiler_params(2),
        name="post_attention",
    )(x, o, mod, w_o, g, w_in, w_out)


def _conv_in_kernel(x_ref, mod_ref, g_ref, w_ref, gate_ref, cu_ref):
    d = D_MODEL
    h = _modulated_norm(x_ref[0], g_ref[...], mod_ref[:, 0:d], mod_ref[:, d:2 * d]).astype(_BF16)
    for j in range(d // MXU_WIDTH):
        lo = j * MXU_WIDTH
        gate_ref[0, :, lo:lo + MXU_WIDTH] = jnp.dot(
            h, w_ref[:, lo:lo + MXU_WIDTH], preferred_element_type=_F32).astype(_BF16)
        c_gate = jnp.dot(h, w_ref[:, d + lo:d + lo + MXU_WIDTH], preferred_element_type=_F32)
        u = jnp.dot(h, w_ref[:, 2 * d + lo:2 * d + lo + MXU_WIDTH], preferred_element_type=_F32)
        cu_ref[0, :, lo:lo + MXU_WIDTH] = (c_gate * u).astype(_BF16)


def _conv_in(x, mod, g, w):
    b, s, d = x.shape
    tm = 512
    tile = pl.BlockSpec((1, tm, d), lambda i, j: (i, j, 0))
    return pl.pallas_call(
        _conv_in_kernel,
        grid=(b, s // tm),
        in_specs=[
            tile,
            pl.BlockSpec((None, 1, 6 * d), lambda i, j: (i, 0, 0)),
            pl.BlockSpec((1, d), lambda i, j: (0, 0)),
            _resident((d, 3 * d), lambda i, j: (0, 0)),
        ],
        out_specs=[tile, tile],
        out_shape=[jax.ShapeDtypeStruct((b, s, d), _BF16)] * 2,
        compiler_params=_compiler_params(2),
        name="conv_in",
    )(x, mod, g, w)


HALO_ROWS = 16


def _conv_out_kernel(x_ref, gate_ref, cu_ref, prev_ref, next_ref, mod_ref, cw_ref, w_o_ref, g_ref,
                     w_in_ref, w_out_ref, gf_ref, out_ref, act_ref, *, tm):
    d = D_MODEL
    j = pl.program_id(1)
    n_sub = tm // TAIL_SUB_ROWS
    row = lax.broadcasted_iota(jnp.int32, (TAIL_SUB_ROWS, d), 0)
    for sub in range(n_sub):
        lo, hi = sub * TAIL_SUB_ROWS, (sub + 1) * TAIL_SUB_ROWS
        rows = slice(lo, hi)
        cu = cu_ref[0, rows, :].astype(_F32)
        if sub == 0:
            prev_row = jnp.where(j > 0, prev_ref[0, HALO_ROWS - 1:HALO_ROWS, :].astype(_F32), 0.0)
        else:
            prev_row = cu_ref[0, lo - 1:lo, :].astype(_F32)
        if sub == n_sub - 1:
            next_row = jnp.where(j < pl.num_programs(1) - 1, next_ref[0, 0:1, :].astype(_F32), 0.0)
        else:
            next_row = cu_ref[0, hi:hi + 1, :].astype(_F32)
        before = jnp.where(row == 0, prev_row, pltpu.roll(cu, 1, 0))
        after = jnp.where(row == TAIL_SUB_ROWS - 1, next_row, pltpu.roll(cu, TAIL_SUB_ROWS - 1, 0))
        z = before * cw_ref[0:1, :] + cu * cw_ref[1:2, :] + after * cw_ref[2:3, :]
        gated = (gate_ref[0, rows, :].astype(_F32) * z).astype(_BF16)
        y = jnp.dot(gated, w_o_ref[...], preferred_element_type=_F32)
        x1 = x_ref[0, rows, :] + mod_ref[:, 2 * d:3 * d] * y
        x2 = _ffn_block(x1, mod_ref, g_ref, w_in_ref, w_out_ref, act_ref, rows)
        out_ref[0, rows, :] = _rms_norm(x2, gf_ref[...])


def _conv_out(x, gate, cu, mod, conv_w, w_o, g, w_in, w_out, final_g):
    b, s, d = x.shape
    f = FFN_HIDDEN
    tm = TAIL_ROWS
    per_tile = tm // HALO_ROWS
    n_halo = s // HALO_ROWS
    tile = pl.BlockSpec((1, tm, d), lambda i, j: (i, j, 0))
    return pl.pallas_call(
        functools.partial(_conv_out_kernel, tm=tm),
        grid=(b, s // tm),
        in_specs=[
            tile, tile, tile,
            pl.BlockSpec((1, HALO_ROWS, d), lambda i, j: (i, jnp.maximum(j * per_tile - 1, 0), 0)),
            pl.BlockSpec((1, HALO_ROWS, d), lambda i, j: (i, jnp.minimum((j + 1) * per_tile, n_halo - 1), 0)),
            pl.BlockSpec((None, 1, 6 * d), lambda i, j: (i, 0, 0)),
            pl.BlockSpec((CONV_WIDTH, d), lambda i, j: (0, 0)),
            _resident((d, d), lambda i, j: (0, 0)),
            pl.BlockSpec((1, d), lambda i, j: (0, 0)),
            _resident((d, 2 * f), lambda i, j: (0, 0)),
            _resident((f, d), lambda i, j: (0, 0)),
            pl.BlockSpec((1, d), lambda i, j: (0, 0)),
        ],
        out_specs=tile,
        out_shape=jax.ShapeDtypeStruct((b, s, d), _F32),
        scratch_shapes=[pltpu.VMEM((tm, f), _BF16)],
        compiler_params=_compiler_params(2),
        name="conv_out",
    )(x, gate, cu, cu, cu, mod, conv_w, w_o, g, w_in, w_out, final_g)


def kernel(x, c, attn_w_qkv, attn_w_o, conv_w_in, conv_w, conv_w_out, ada_w, ada_b,
           norm_mix_g, norm_ffn_g, ffn_w_in, ffn_w_out, final_g):
    b, s, d = x.shape
    assert d == D_MODEL and attn_w_qkv.shape[0] == 1 and conv_w_in.shape[0] == 1 and ada_w.shape[0] == 2
    mod = _modulation(c, ada_w, ada_b).reshape(2, b, 1, 6 * d)

    *qkv, w_o, w_in0, w_out0 = _qkv(x, mod[0], norm_mix_g[0:1], attn_w_qkv[0].astype(_BF16),
                                    [(attn_w_o, 0), (ffn_w_in, 0), (ffn_w_out, 0)])
    o, w_ci, w_co, w_in1, w_out1 = _attention(
        qkv, s, [(conv_w_in, 0), (conv_w_out, 0), (ffn_w_in, 1), (ffn_w_out, 1)])
    x = _post_attention(x, o, mod[0], w_o, norm_ffn_g[0:1], w_in0, w_out0)

    gate, cu = _conv_in(x, mod[1], norm_mix_g[1:2], w_ci)
    return _conv_out(x, gate, cu, mod[1], conv_w[0], w_co, norm_ffn_g[1:2],
                     w_in1, w_out1, final_g.reshape(1, d))
```

```python
import functools

import numpy as np
import jax
import jax.numpy as jnp
from jax import lax
from jax.experimental import pallas as pl
from jax.experimental.pallas import tpu as pltpu

D_MODEL = 1024
HEAD_DIM = 64
N_HEADS = D_MODEL // HEAD_DIM
DILATED_GROUPS = ((128, 1), (512, 4), (2048, 16))
ROPE_THETA = 500000.0
ROPE_DIM = HEAD_DIM // 4
CONV_WIDTH = 3
FFN_HIDDEN = -(-8 * D_MODEL // (3 * 256)) * 256
NORM_EPS = 1e-6
NEG_INF = -1e30

LANES = 128
MXU_WIDTH = 256
HALF_WINDOW = 64
Q_BLOCK = 128
SCORE_SCALE = HEAD_DIM ** -0.5 * float(np.log2(np.e))
VMEM_LIMIT_BYTES = 56 * 1024 * 1024

assert all(w // (2 * d) == HALF_WINDOW for w, d in DILATED_GROUPS)

_F32 = jnp.float32
_BF16 = jnp.bfloat16


def _compiler_params(n_grid_dims):
    return pltpu.CompilerParams(
        dimension_semantics=("arbitrary",) * n_grid_dims,
        vmem_limit_bytes=VMEM_LIMIT_BYTES,
    )


def _resident(block_shape, index_map):
    return pl.BlockSpec(block_shape, index_map, pipeline_mode=pl.Buffered(1))


def _rms_norm(x, g):
    r = lax.rsqrt(jnp.mean(x * x, axis=-1, keepdims=True) + NORM_EPS)
    return (x * r) * g


def _modulated_norm(x, g, shift, scale):
    return _rms_norm(x, g) * (1.0 + scale) + shift


def _modulation_kernel(c_ref, w_ref, b_ref, o_ref):
    c = c_ref[...]
    cond = (c * jax.nn.sigmoid(c)).astype(_BF16)
    o_ref[...] = jnp.dot(cond, w_ref[...].astype(_BF16), preferred_element_type=_F32) + b_ref[...]


def _modulation(c, ada_w, ada_b):
    depth, d, n = ada_w.shape
    b = c.shape[0]
    tn = 1536
    return pl.pallas_call(
        _modulation_kernel,
        grid=(depth, n // tn),
        in_specs=[
            pl.BlockSpec((b, d), lambda i, j: (0, 0)),
            pl.BlockSpec((None, d, tn), lambda i, j: (i, 0, j)),
            pl.BlockSpec((None, 1, tn), lambda i, j: (i, 0, j)),
        ],
        out_specs=pl.BlockSpec((None, b, tn), lambda i, j: (i, 0, j)),
        out_shape=jax.ShapeDtypeStruct((depth, b, n), _F32),
        compiler_params=_compiler_params(2),
        name="modulation",
    )(c, ada_w, ada_b.reshape(depth, 1, n))


CAST_ROWS = 16


def _cast_specs(weights, n_inner, total_steps):
    in_specs, out_specs, out_shapes, args, pers = [], [], [], [], []
    for w, layer in weights:
        rows, cols = w.shape[1:]
        n_blocks = max(nb for nb in range(1, total_steps + 1)
                       if total_steps % nb == 0 and rows % (nb * CAST_ROWS) == 0)
        block_rows = rows // n_blocks
        block = lambda i, j, per=total_steps // n_blocks: (i * n_inner + j) // per
        in_specs.append(pl.BlockSpec((None, block_rows, cols),
                                     lambda i, j, layer=layer, block=block: (layer, block(i, j), 0)))
        out_specs.append(pl.BlockSpec((block_rows, cols), lambda i, j, block=block: (block(i, j), 0)))
        out_shapes.append(jax.ShapeDtypeStruct((rows, cols), _BF16))
        args.append(w)
        pers.append(total_steps // n_blocks)
    return in_specs, out_specs, out_shapes, args, tuple(pers)


CAST_GUARD_STEPS = 4


def _cast_blocks(in_refs, out_refs, pers, step, guarded):
    for w_ref, o_ref, per in zip(in_refs, out_refs, pers):
        if guarded and per >= CAST_GUARD_STEPS:
            @pl.when(step % per == 0)
            def _(w_ref=w_ref, o_ref=o_ref):
                o_ref[...] = w_ref[...].astype(_BF16)
        elif not guarded and per < CAST_GUARD_STEPS:
            o_ref[...] = w_ref[...].astype(_BF16)


def _rope_tables(seq, dilation, q_scale):
    half = ROPE_DIM // 2
    length = seq // dilation
    inv = ROPE_THETA ** (-np.arange(half, dtype=np.float64) * (2.0 / ROPE_DIM))
    pos = (np.arange(length)[None, :] * dilation + np.arange(dilation)[:, None]).astype(np.float64)
    ang = pos[:, :, None] * inv[None, None, :]
    cos, sin = np.cos(ang), np.sin(ang)
    c = np.ones((dilation, length, HEAD_DIM))
    sa = np.zeros((dilation, length, HEAD_DIM))
    sb = np.zeros((dilation, length, HEAD_DIM))
    c[:, :, :half] = cos
    c[:, :, half:ROPE_DIM] = cos
    sa[:, :, :half] = -sin
    sb[:, :, half:ROPE_DIM] = sin
    k_tab = np.stack([np.tile(t, (1, 1, LANES // HEAD_DIM)) for t in (c, sa, sb)])
    return np.stack([k_tab * q_scale, k_tab]).astype(np.float32)


QKV_ROWS = 256


def _qkv_kernel(x_ref, mod_ref, g_ref, w_ref, tab1_ref, tab4_ref, tab16_ref, *refs, cast_pers):
    n_cast = len(cast_pers)
    cast_in, (o1_ref, o4_ref, o16_ref) = refs[:n_cast], refs[n_cast:n_cast + 3]
    cast_out, (hs1_ref, hs4_ref) = refs[n_cast + 3:2 * n_cast + 3], refs[2 * n_cast + 3:]
    step = pl.program_id(0) * pl.num_programs(1) + pl.program_id(1)
    _cast_blocks(cast_in, cast_out, cast_pers, step, True)
    d, tm = D_MODEL, QKV_ROWS
    d4 = DILATED_GROUPS[1][1]
    n_slabs = d // LANES
    h = _modulated_norm(x_ref[0], g_ref[...], mod_ref[:, 0:d], mod_ref[:, d:2 * d])
    for c in range(n_slabs):
        hs1_ref[c] = h[:, c * LANES:(c + 1) * LANES]
    rows4 = tm // d4
    for c in range(n_slabs):
        hs4_ref[c] = jnp.concatenate([hs1_ref[c, pl.ds(r, rows4, stride=d4), :] for r in range(d4)], axis=0)
    h4 = jnp.concatenate([hs4_ref[c] for c in range(n_slabs)], axis=1)
    rows16 = rows4 // d4
    h16 = jnp.concatenate(
        [jnp.concatenate([hs4_ref[c, pl.ds((r16 % d4) * rows4 + r16 // d4, rows16, stride=d4), :]
                          for c in range(n_slabs)], axis=1)
         for r16 in range(d4 * d4)], axis=0)

    groups = ((h, tab1_ref, o1_ref), (h4, tab4_ref, o4_ref), (h16, tab16_ref, o16_ref))
    for g, (hg, tab_ref, o_ref) in enumerate(groups):
        dilation = DILATED_GROUPS[g][1]
        rows = tm // dilation
        hb = hg.astype(_BF16)
        if g == 1:
            _cast_blocks(cast_in, cast_out, cast_pers, step, False)
        for part in range(3):
            for j in range(d // MXU_WIDTH):
                col = part * d + j * MXU_WIDTH
                res = jnp.dot(hb, w_ref[:, g * 3 * d + col:g * 3 * d + col + MXU_WIDTH],
                              preferred_element_type=_F32)
                if part < 2:
                    halves = []
                    for s in range(MXU_WIDTH // LANES):
                        xs = res[:, s * LANES:(s + 1) * LANES]
                        tabs = [jnp.concatenate([tab_ref[part, t, r] for r in range(dilation)], axis=0)
                                for t in range(3)]
                        halves.append(xs * tabs[0]
                                      + pltpu.roll(xs, LANES - ROPE_DIM // 2, 1) * tabs[1]
                                      + pltpu.roll(xs, ROPE_DIM // 2, 1) * tabs[2])
                    res = jnp.concatenate(halves, axis=1)
                res = res.astype(_BF16)
                for r in range(dilation):
                    o_ref[0, r, :, col:col + MXU_WIDTH] = res[r * rows:(r + 1) * rows]


def _qkv(x, mod, g, w, cast_weights):
    b, s, d = x.shape
    tm = QKV_ROWS
    dils = [dil for _, dil in DILATED_GROUPS]
    assert dils == [1, 4, 16]
    tabs = [jnp.asarray(_rope_tables(s, dil, SCORE_SCALE)) for dil in dils]
    c_in, c_out, c_shapes, c_args, c_pers = _cast_specs(cast_weights, s // tm, b * (s // tm))
    return pl.pallas_call(
        functools.partial(_qkv_kernel, cast_pers=c_pers),
        grid=(b, s // tm),
        in_specs=[
            pl.BlockSpec((1, tm, d), lambda i, j: (i, j, 0)),
            pl.BlockSpec((None, 1, 6 * d), lambda i, j: (i, 0, 0)),
            pl.BlockSpec((1, d), lambda i, j: (0, 0)),
            _resident(w.shape, lambda i, j: (0, 0)),
        ] + [pl.BlockSpec((2, 3, dil, tm // dil, LANES), lambda i, j: (0, 0, 0, j, 0)) for dil in dils] + c_in,
        out_specs=[pl.BlockSpec((1, dil, tm // dil, 3 * d), lambda i, j: (i, 0, j, 0)) for dil in dils] + c_out,
        out_shape=[jax.ShapeDtypeStruct((b, dil, s // dil, 3 * d), _BF16) for dil in dils] + c_shapes,
        scratch_shapes=[pltpu.VMEM((d // LANES, tm, LANES), _F32)] * 2,
        compiler_params=_compiler_params(2),
        name="qkv",
    )(x, mod, g, w, *tabs, *c_args)


def _band_bias(n_keys):
    i = np.arange(Q_BLOCK)[:, None]
    j = np.arange(n_keys)[None, :]
    one = np.stack([np.where(np.abs(off + i - j) <= HALF_WINDOW, 0.0, NEG_INF)
                    for off in (0, HALF_WINDOW, 2 * HALF_WINDOW)])
    return np.concatenate([one, one], axis=1).astype(np.float32)


def _pair_bias():
    band = _band_bias(Q_BLOCK)[0]
    masked = np.full_like(band, NEG_INF)
    return np.stack([np.concatenate([band, masked], axis=1), np.concatenate([masked, band], axis=1)])


ATTN_UNROLL = 16
ATTN_PAIRS = 2


def _attention_kernel(q0, k0, v0, q1, k1, v1, q2, k2, v2, bias_w_ref, bias_p_ref, *refs, seq, cast_pers):
    n_cast = len(cast_pers)
    cast_in, o_ref = refs[:n_cast], refs[n_cast]
    cast_out, (s4_ref, s1_ref) = refs[n_cast + 1:2 * n_cast + 1], refs[2 * n_cast + 1:]
    step = pl.program_id(0) * pl.num_programs(1) + pl.program_id(1)
    _cast_blocks(cast_in, cast_out, cast_pers, step, True)
    lane = lax.broadcasted_iota(jnp.int32, (Q_BLOCK, LANES), 1)
    head_a = lane < HEAD_DIM
    for pair in range(ATTN_PAIRS):
        if pair == 1:
            _cast_blocks(cast_in, cast_out, cast_pers, step, False)
        _attention_pair(q0, k0, v0, q1, k1, v1, q2, k2, v2, bias_w_ref, bias_p_ref, o_ref,
                        s4_ref.at[pair], s1_ref.at[pair], pl.ds(pair * LANES, LANES), head_a, seq)


def _attention_pair(q0, k0, v0, q1, k1, v1, q2, k2, v2, bias_w_ref, bias_p_ref, o_ref,
                    s4_ref, s1_ref, lanes, head_a, seq):
    d4 = DILATED_GROUPS[1][1]
    len4 = seq // d4

    def block_stats(q, k, v, bias):
        n_keys = k.shape[0]
        zero = jnp.zeros_like(q)
        q2 = jnp.concatenate([jnp.where(head_a, q, zero), jnp.where(head_a, zero, q)], axis=0)
        s = lax.dot_general(q2, k, (((1,), (1,)), ((), ())), preferred_element_type=_F32) + bias
        m = jnp.max(s, axis=-1, keepdims=True)
        p = jnp.exp2(s - m).astype(_BF16)
        v_ext = jnp.concatenate([v, jnp.ones((n_keys, LANES), _BF16)], axis=1)
        pv = jnp.dot(p, v_ext, preferred_element_type=_F32)
        m_pair = jnp.where(head_a, jnp.broadcast_to(m[:Q_BLOCK], (Q_BLOCK, LANES)),
                           jnp.broadcast_to(m[Q_BLOCK:], (Q_BLOCK, LANES)))
        l_pair = jnp.where(head_a, pv[:Q_BLOCK, LANES:], pv[Q_BLOCK:, LANES:])
        acc_pair = jnp.where(head_a, pv[:Q_BLOCK, :LANES], pv[Q_BLOCK:, :LANES])
        return m_pair, l_pair, acc_pair

    def merge(old, new):
        m = jnp.maximum(old[0], new[0])
        a_old = jnp.exp2(old[0] - m)
        a_new = jnp.exp2(new[0] - m)
        return m, a_old * old[1] + a_new * new[1], a_old * old[2] + a_new * new[2]

    def load_block(q_ref, k_ref, v_ref, r, j, length):
        n_keys = min(length, Q_BLOCK + 2 * HALF_WINDOW)
        m0 = pl.multiple_of(j * Q_BLOCK, Q_BLOCK)
        k_start = pl.multiple_of(jnp.clip(m0 - HALF_WINDOW, 0, length - n_keys), HALF_WINDOW)
        bias = bias_w_ref[(m0 - k_start) // HALF_WINDOW]
        return (q_ref[0, r, pl.ds(m0, Q_BLOCK), lanes], k_ref[0, r, pl.ds(k_start, n_keys), lanes],
                v_ref[0, r, pl.ds(k_start, n_keys), lanes], bias)

    def load_pair_block(q_ref, k_ref, v_ref, r):
        first = pl.multiple_of((r // 2) * 2, 2)
        keys = lambda ref: ref[0, pl.ds(first, 2), :, lanes].reshape(2 * Q_BLOCK, LANES)
        return q_ref[0, r, :, lanes], keys(k_ref), keys(v_ref), bias_p_ref[r % 2]

    def group16(idx, carry):
        a, c = idx // d4, idx % d4
        state = block_stats(*load_pair_block(q2, k2, v2, idx))
        rows = pl.ds(c * len4 + a, Q_BLOCK, stride=d4)
        for t in range(3):
            s4_ref[t, rows, :] = state[t]
        return carry

    def group4(idx, carry):
        c, j = idx // (len4 // Q_BLOCK), idx % (len4 // Q_BLOCK)
        new = block_stats(*load_block(q1, k1, v1, c, j, len4))
        src = pl.ds(pl.multiple_of(c * len4 + j * Q_BLOCK, Q_BLOCK), Q_BLOCK)
        state = merge([s4_ref[t, src, :] for t in range(3)], new)
        rows = pl.ds(j * Q_BLOCK * d4 + c, Q_BLOCK, stride=d4)
        for t in range(3):
            s1_ref[t, rows, :] = state[t]
        return carry

    def group1(j, carry):
        new = block_stats(*load_block(q0, k0, v0, 0, j, seq))
        rows = pl.ds(pl.multiple_of(j * Q_BLOCK, Q_BLOCK), Q_BLOCK)
        _, l, acc = merge([s1_ref[t, rows, :] for t in range(3)], new)
        o_ref[0, rows, lanes] = (acc / l).astype(o_ref.dtype)
        return carry

    n_blocks = seq // Q_BLOCK
    lax.fori_loop(0, n_blocks, group16, 0, unroll=ATTN_UNROLL)
    lax.fori_loop(0, n_blocks, group4, 0, unroll=ATTN_UNROLL)
    lax.fori_loop(0, n_blocks, group1, 0, unroll=ATTN_UNROLL)


def _attention(qkv_groups, seq, cast_weights):
    b = qkv_groups[0].shape[0]
    d = D_MODEL
    width = ATTN_PAIRS * LANES
    n_steps = d // width
    in_specs, args = [], []
    for qkv, (_, dil) in zip(qkv_groups, DILATED_GROUPS):
        for part in range(3):
            in_specs.append(pl.BlockSpec((1, dil, seq // dil, width),
                                         lambda i, p, part=part: (i, 0, 0, part * n_steps + p)))
            args.append(qkv)
    wide = Q_BLOCK + 2 * HALF_WINDOW
    assert seq // DILATED_GROUPS[2][1] == Q_BLOCK and seq // DILATED_GROUPS[1][1] >= wide
    in_specs += [pl.BlockSpec((3, 2 * Q_BLOCK, wide), lambda i, p: (0, 0, 0)),
                 pl.BlockSpec((2, 2 * Q_BLOCK, wide), lambda i, p: (0, 0, 0))]
    args += [jnp.asarray(_band_bias(wide)), jnp.asarray(_pair_bias())]
    c_in, c_out, c_shapes, c_args, c_pers = _cast_specs(cast_weights, n_steps, b * n_steps)
    return pl.pallas_call(
        functools.partial(_attention_kernel, seq=seq, cast_pers=c_pers),
        grid=(b, n_steps),
        in_specs=in_specs + c_in,
        out_specs=[pl.BlockSpec((1, seq, width), lambda i, p: (i, 0, p))] + c_out,
        out_shape=[jax.ShapeDtypeStruct((b, seq, d), _BF16)] + c_shapes,
        scratch_shapes=[pltpu.VMEM((ATTN_PAIRS, 3, seq, LANES), _F32),
                        pltpu.VMEM((ATTN_PAIRS, 3, seq, LANES), _F32)],
        compiler_params=_compiler_params(2),
        name="dilated_attention",
    )(*args, *c_args)


FFN_CHUNK = MXU_WIDTH


TAIL_ROWS = 1024
TAIL_SUB_ROWS = 256


def _ffn_block(x1, mod_ref, g_ref, w_in_ref, w_out_ref, act_ref, rows):
    d, f = D_MODEL, FFN_HIDDEN
    h = _modulated_norm(x1, g_ref[...], mod_ref[:, 3 * d:4 * d], mod_ref[:, 4 * d:5 * d]).astype(_BF16)
    for c in range(f // FFN_CHUNK):
        lo = c * FFN_CHUNK
        gate = jnp.dot(h, w_in_ref[:, lo:lo + FFN_CHUNK], preferred_element_type=_F32)
        up = jnp.dot(h, w_in_ref[:, f + lo:f + lo + FFN_CHUNK], preferred_element_type=_F32)
        act_ref[rows, lo:lo + FFN_CHUNK] = (gate * jax.nn.sigmoid(gate) * up).astype(_BF16)
    y = jnp.dot(act_ref[rows, :], w_out_ref[...], preferred_element_type=_F32)
    return x1 + mod_ref[:, 5 * d:6 * d] * y


def _post_attention_kernel(x_ref, o_ref, mod_ref, w_o_ref, g_ref, w_in_ref, w_out_ref, out_ref, act_ref):
    d = D_MODEL
    for sub in range(TAIL_ROWS // TAIL_SUB_ROWS):
        rows = slice(sub * TAIL_SUB_ROWS, (sub + 1) * TAIL_SUB_ROWS)
        y = jnp.dot(o_ref[0, rows, :], w_o_ref[...], preferred_element_type=_F32)
        x1 = x_ref[0, rows, :] + mod_ref[:, 2 * d:3 * d] * y
        out_ref[0, rows, :] = _ffn_block(x1, mod_ref, g_ref, w_in_ref, w_out_ref, act_ref, rows)


def _post_attention(x, o, mod, w_o, g, w_in, w_out):
    b, s, d = x.shape
    f = FFN_HIDDEN
    tm = TAIL_ROWS
    return pl.pallas_call(
        _post_attention_kernel,
        grid=(b, s // tm),
        in_specs=[
            pl.BlockSpec((1, tm, d), lambda i, j: (i, j, 0)),
            pl.BlockSpec((1, tm, d), lambda i, j: (i, j, 0)),
            pl.BlockSpec((None, 1, 6 * d), lambda i, j: (i, 0, 0)),
            _resident((d, d), lambda i, j: (0, 0)),
            pl.BlockSpec((1, d), lambda i, j: (0, 0)),
            _resident((d, 2 * f), lambda i, j: (0, 0)),
            _resident((f, d), lambda i, j: (0, 0)),
        ],
        out_specs=pl.BlockSpec((1, tm, d), lambda i, j: (i, j, 0)),
        out_shape=jax.ShapeDtypeStruct((b, s, d), _F32),
        scratch_shapes=[pltpu.VMEM((tm, f), _BF16)],
        compiler_params=_compiler_params(2),
        name="post_attention",
    )(x, o, mod, w_o, g, w_in, w_out)


CONV_IN_ROWS = 1024
CONV_IN_SUB_ROWS = 512


def _conv_in_kernel(x_ref, mod_ref, g_ref, w_ref, gate_ref, cu_ref):
    d = D_MODEL
    for sub in range(CONV_IN_ROWS // CONV_IN_SUB_ROWS):
        rows = slice(sub * CONV_IN_SUB_ROWS, (sub + 1) * CONV_IN_SUB_ROWS)
        h = _modulated_norm(x_ref[0, rows, :], g_ref[...], mod_ref[:, 0:d], mod_ref[:, d:2 * d]).astype(_BF16)
        for j in range(d // MXU_WIDTH):
            lo = j * MXU_WIDTH
            gate_ref[0, rows, lo:lo + MXU_WIDTH] = jnp.dot(
                h, w_ref[:, lo:lo + MXU_WIDTH], preferred_element_type=_F32).astype(_BF16)
            c_gate = jnp.dot(h, w_ref[:, d + lo:d + lo + MXU_WIDTH], preferred_element_type=_F32)
            u = jnp.dot(h, w_ref[:, 2 * d + lo:2 * d + lo + MXU_WIDTH], preferred_element_type=_F32)
            cu_ref[0, rows, lo:lo + MXU_WIDTH] = (c_gate * u).astype(_BF16)


def _conv_in(x, mod, g, w):
    b, s, d = x.shape
    tm = CONV_IN_ROWS
    tile = pl.BlockSpec((1, tm, d), lambda i, j: (i, j, 0))
    return pl.pallas_call(
        _conv_in_kernel,
        grid=(b, s // tm),
        in_specs=[
            tile,
            pl.BlockSpec((None, 1, 6 * d), lambda i, j: (i, 0, 0)),
            pl.BlockSpec((1, d), lambda i, j: (0, 0)),
            _resident((d, 3 * d), lambda i, j: (0, 0)),
        ],
        out_specs=[tile, tile],
        out_shape=[jax.ShapeDtypeStruct((b, s, d), _BF16)] * 2,
        compiler_params=_compiler_params(2),
        name="conv_in",
    )(x, mod, g, w)


HALO_ROWS = 16


def _conv_out_kernel(x_ref, gate_ref, cu_ref, prev_ref, next_ref, mod_ref, cw_ref, w_o_ref, g_ref,
                     w_in_ref, w_out_ref, gf_ref, out_ref, act_ref, *, tm):
    d = D_MODEL
    j = pl.program_id(1)
    n_sub = tm // TAIL_SUB_ROWS
    row = lax.broadcasted_iota(jnp.int32, (TAIL_SUB_ROWS, d), 0)
    for sub in range(n_sub):
        lo, hi = sub * TAIL_SUB_ROWS, (sub + 1) * TAIL_SUB_ROWS
        rows = slice(lo, hi)
        cu = cu_ref[0, rows, :].astype(_F32)
        if sub == 0:
            prev_row = jnp.where(j > 0, prev_ref[0, HALO_ROWS - 1:HALO_ROWS, :].astype(_F32), 0.0)
        else:
            prev_row = cu_ref[0, lo - 1:lo, :].astype(_F32)
        if sub == n_sub - 1:
            next_row = jnp.where(j < pl.num_programs(1) - 1, next_ref[0, 0:1, :].astype(_F32), 0.0)
        else:
            next_row = cu_ref[0, hi:hi + 1, :].astype(_F32)
        before = jnp.where(row == 0, prev_row, pltpu.roll(cu, 1, 0))
        after = jnp.where(row == TAIL_SUB_ROWS - 1, next_row, pltpu.roll(cu, TAIL_SUB_ROWS - 1, 0))
        z = before * cw_ref[0:1, :] + cu * cw_ref[1:2, :] + after * cw_ref[2:3, :]
        gated = (gate_ref[0, rows, :].astype(_F32) * z).astype(_BF16)
        y = jnp.dot(gated, w_o_ref[...], preferred_element_type=_F32)
        x1 = x_ref[0, rows, :] + mod_ref[:, 2 * d:3 * d] * y
        x2 = _ffn_block(x1, mod_ref, g_ref, w_in_ref, w_out_ref, act_ref, rows)
        out_ref[0, rows, :] = _rms_norm(x2, gf_ref[...])


def _conv_out(x, gate, cu, mod, conv_w, w_o, g, w_in, w_out, final_g):
    b, s, d = x.shape
    f = FFN_HIDDEN
    tm = TAIL_ROWS
    per_tile = tm // HALO_ROWS
    n_halo = s // HALO_ROWS
    tile = pl.BlockSpec((1, tm, d), lambda i, j: (i, j, 0))
    return pl.pallas_call(
        functools.partial(_conv_out_kernel, tm=tm),
        grid=(b, s // tm),
        in_specs=[
            tile, tile, tile,
            pl.BlockSpec((1, HALO_ROWS, d), lambda i, j: (i, jnp.maximum(j * per_tile - 1, 0), 0)),
            pl.BlockSpec((1, HALO_ROWS, d), lambda i, j: (i, jnp.minimum((j + 1) * per_tile, n_halo - 1), 0)),
            pl.BlockSpec((None, 1, 6 * d), lambda i, j: (i, 0, 0)),
            pl.BlockSpec((CONV_WIDTH, d), lambda i, j: (0, 0)),
            _resident((d, d), lambda i, j: (0, 0)),
            pl.BlockSpec((1, d), lambda i, j: (0, 0)),
            _resident((d, 2 * f), lambda i, j: (0, 0)),
            _resident((f, d), lambda i, j: (0, 0)),
            pl.BlockSpec((1, d), lambda i, j: (0, 0)),
        ],
        out_specs=tile,
        out_shape=jax.ShapeDtypeStruct((b, s, d), _F32),
        scratch_shapes=[pltpu.VMEM((tm, f), _BF16)],
        compiler_params=_compiler_params(2),
        name="conv_out",
    )(x, gate, cu, cu, cu, mod, conv_w, w_o, g, w_in, w_out, final_g)


def kernel(x, c, attn_w_qkv, attn_w_o, conv_w_in, conv_w, conv_w_out, ada_w, ada_b,
           norm_mix_g, norm_ffn_g, ffn_w_in, ffn_w_out, final_g):
    b, s, d = x.shape
    assert d == D_MODEL and attn_w_qkv.shape[0] == 1 and conv_w_in.shape[0] == 1 and ada_w.shape[0] == 2
    mod = _modulation(c, ada_w, ada_b).reshape(2, b, 1, 6 * d)

    *qkv, w_o, w_in0, w_out0 = _qkv(x, mod[0], norm_mix_g[0:1], attn_w_qkv[0].astype(_BF16),
                                    [(attn_w_o, 0), (ffn_w_in, 0), (ffn_w_out, 0)])
    o, w_ci, w_co, w_in1, w_out1 = _attention(
        qkv, s, [(conv_w_in, 0), (conv_w_out, 0), (ffn_w_in, 1), (ffn_w_out, 1)])
    x = _post_attention(x, o, mod[0], w_o, norm_ffn_g[0:1], w_in0, w_out0)

    gate, cu = _conv_in(x, mod[1], norm_mix_g[1:2], w_ci)
    return _conv_out(x, gate, cu, mod[1], conv_w[0], w_co, norm_ffn_g[1:2],
                     w_in1, w_out1, final_g.reshape(1, d))
```

```python
import functools

import numpy as np
import jax
import jax.numpy as jnp
from jax import lax
from jax.experimental import pallas as pl
from jax.experimental.pallas import tpu as pltpu

D_MODEL = 1024
HEAD_DIM = 64
N_HEADS = D_MODEL // HEAD_DIM
DILATED_GROUPS = ((128, 1), (512, 4), (2048, 16))
ROPE_THETA = 500000.0
ROPE_DIM = HEAD_DIM // 4
CONV_WIDTH = 3
FFN_HIDDEN = -(-8 * D_MODEL // (3 * 256)) * 256
NORM_EPS = 1e-6
NEG_INF = -1e30

LANES = 128
MXU_WIDTH = 256
HALF_WINDOW = 64
Q_BLOCK = 128
SCORE_SCALE = HEAD_DIM ** -0.5 * float(np.log2(np.e))
VMEM_LIMIT_BYTES = 56 * 1024 * 1024

assert all(w // (2 * d) == HALF_WINDOW for w, d in DILATED_GROUPS)

_F32 = jnp.float32
_BF16 = jnp.bfloat16


def _compiler_params(n_grid_dims):
    return pltpu.CompilerParams(
        dimension_semantics=("arbitrary",) * n_grid_dims,
        vmem_limit_bytes=VMEM_LIMIT_BYTES,
    )


def _resident(block_shape, index_map):
    return pl.BlockSpec(block_shape, index_map, pipeline_mode=pl.Buffered(1))


def _rms_norm(x, g):
    r = lax.rsqrt(jnp.mean(x * x, axis=-1, keepdims=True) + NORM_EPS)
    return (x * r) * g


def _modulated_norm(x, g, shift, scale):
    return _rms_norm(x, g) * (1.0 + scale) + shift


def _modulation_kernel(c_ref, w_ref, b_ref, o_ref):
    c = c_ref[...]
    cond = (c * jax.nn.sigmoid(c)).astype(_BF16)
    o_ref[...] = jnp.dot(cond, w_ref[...].astype(_BF16), preferred_element_type=_F32) + b_ref[...]


def _modulation(c, ada_w, ada_b):
    depth, d, n = ada_w.shape
    b = c.shape[0]
    tn = 1536
    return pl.pallas_call(
        _modulation_kernel,
        grid=(depth, n // tn),
        in_specs=[
            pl.BlockSpec((b, d), lambda i, j: (0, 0)),
            pl.BlockSpec((None, d, tn), lambda i, j: (i, 0, j)),
            pl.BlockSpec((None, 1, tn), lambda i, j: (i, 0, j)),
        ],
        out_specs=pl.BlockSpec((None, b, tn), lambda i, j: (i, 0, j)),
        out_shape=jax.ShapeDtypeStruct((depth, b, n), _F32),
        compiler_params=_compiler_params(2),
        name="modulation",
    )(c, ada_w, ada_b.reshape(depth, 1, n))


CAST_ROWS = 16


def _cast_specs(weights, n_inner, total_steps):
    in_specs, out_specs, out_shapes, args, pers = [], [], [], [], []
    for w, layer in weights:
        rows, cols = w.shape[1:]
        n_blocks = max(nb for nb in range(1, total_steps + 1)
                       if total_steps % nb == 0 and rows % (nb * CAST_ROWS) == 0)
        block_rows = rows // n_blocks
        block = lambda i, j, per=total_steps // n_blocks: (i * n_inner + j) // per
        in_specs.append(pl.BlockSpec((None, block_rows, cols),
                                     lambda i, j, layer=layer, block=block: (layer, block(i, j), 0)))
        out_specs.append(pl.BlockSpec((block_rows, cols), lambda i, j, block=block: (block(i, j), 0)))
        out_shapes.append(jax.ShapeDtypeStruct((rows, cols), _BF16))
        args.append(w)
        pers.append(total_steps // n_blocks)
    return in_specs, out_specs, out_shapes, args, tuple(pers)


CAST_GUARD_STEPS = 4


def _cast_blocks(in_refs, out_refs, pers, step, guarded):
    for w_ref, o_ref, per in zip(in_refs, out_refs, pers):
        if guarded and per >= CAST_GUARD_STEPS:
            @pl.when(step % per == 0)
            def _(w_ref=w_ref, o_ref=o_ref):
                o_ref[...] = w_ref[...].astype(_BF16)
        elif not guarded and per < CAST_GUARD_STEPS:
            o_ref[...] = w_ref[...].astype(_BF16)


def _rope_tables(seq, dilation, q_scale):
    half = ROPE_DIM // 2
    length = seq // dilation
    inv = ROPE_THETA ** (-np.arange(half, dtype=np.float64) * (2.0 / ROPE_DIM))
    pos = (np.arange(length)[None, :] * dilation + np.arange(dilation)[:, None]).astype(np.float64)
    ang = pos[:, :, None] * inv[None, None, :]
    cos, sin = np.cos(ang), np.sin(ang)
    c = np.ones((dilation, length, HEAD_DIM))
    sa = np.zeros((dilation, length, HEAD_DIM))
    sb = np.zeros((dilation, length, HEAD_DIM))
    c[:, :, :half] = cos
    c[:, :, half:ROPE_DIM] = cos
    sa[:, :, :half] = -sin
    sb[:, :, half:ROPE_DIM] = sin
    k_tab = np.stack([np.tile(t, (1, 1, LANES // HEAD_DIM)) for t in (c, sa, sb)])
    return np.stack([k_tab * q_scale, k_tab]).astype(np.float32)


QKV_ROWS = 256


def _qkv_kernel(x_ref, mod_ref, g_ref, w_ref, tab1_ref, tab4_ref, tab16_ref, *refs, cast_pers):
    n_cast = len(cast_pers)
    cast_in, (o1_ref, o4_ref, o16_ref) = refs[:n_cast], refs[n_cast:n_cast + 3]
    cast_out, (hs1_ref, hs4_ref) = refs[n_cast + 3:2 * n_cast + 3], refs[2 * n_cast + 3:]
    step = pl.program_id(0) * pl.num_programs(1) + pl.program_id(1)
    _cast_blocks(cast_in, cast_out, cast_pers, step, True)
    d, tm = D_MODEL, QKV_ROWS
    d4 = DILATED_GROUPS[1][1]
    n_slabs = d // LANES
    h = _modulated_norm(x_ref[0], g_ref[...], mod_ref[:, 0:d], mod_ref[:, d:2 * d])
    for c in range(n_slabs):
        hs1_ref[c] = h[:, c * LANES:(c + 1) * LANES]
    rows4 = tm // d4
    for c in range(n_slabs):
        hs4_ref[c] = jnp.concatenate([hs1_ref[c, pl.ds(r, rows4, stride=d4), :] for r in range(d4)], axis=0)
    h4 = jnp.concatenate([hs4_ref[c] for c in range(n_slabs)], axis=1)
    rows16 = rows4 // d4
    h16 = jnp.concatenate(
        [jnp.concatenate([hs4_ref[c, pl.ds((r16 % d4) * rows4 + r16 // d4, rows16, stride=d4), :]
                          for c in range(n_slabs)], axis=1)
         for r16 in range(d4 * d4)], axis=0)

    groups = ((h, tab1_ref, o1_ref), (h4, tab4_ref, o4_ref), (h16, tab16_ref, o16_ref))
    for g, (hg, tab_ref, o_ref) in enumerate(groups):
        dilation = DILATED_GROUPS[g][1]
        rows = tm // dilation
        hb = hg.astype(_BF16)
        if g == 1:
            _cast_blocks(cast_in, cast_out, cast_pers, step, False)
        for part in range(3):
            for j in range(d // MXU_WIDTH):
                col = part * d + j * MXU_WIDTH
                res = jnp.dot(hb, w_ref[:, g * 3 * d + col:g * 3 * d + col + MXU_WIDTH],
                              preferred_element_type=_F32)
                if part < 2:
                    halves = []
                    for s in range(MXU_WIDTH // LANES):
                        xs = res[:, s * LANES:(s + 1) * LANES]
                        tabs = [jnp.concatenate([tab_ref[part, t, r] for r in range(dilation)], axis=0)
                                for t in range(3)]
                        halves.append(xs * tabs[0]
                                      + pltpu.roll(xs, LANES - ROPE_DIM // 2, 1) * tabs[1]
                                      + pltpu.roll(xs, ROPE_DIM // 2, 1) * tabs[2])
                    res = jnp.concatenate(halves, axis=1)
                res = res.astype(_BF16)
                for r in range(dilation):
                    o_ref[0, r, :, col:col + MXU_WIDTH] = res[r * rows:(r + 1) * rows]


def _qkv(x, mod, g, w, cast_weights):
    b, s, d = x.shape
    tm = QKV_ROWS
    dils = [dil for _, dil in DILATED_GROUPS]
    assert dils == [1, 4, 16]
    tabs = [jnp.asarray(_rope_tables(s, dil, SCORE_SCALE)) for dil in dils]
    c_in, c_out, c_shapes, c_args, c_pers = _cast_specs(cast_weights, s // tm, b * (s // tm))
    return pl.pallas_call(
        functools.partial(_qkv_kernel, cast_pers=c_pers),
        grid=(b, s // tm),
        in_specs=[
            pl.BlockSpec((1, tm, d), lambda i, j: (i, j, 0)),
            pl.BlockSpec((None, 1, 6 * d), lambda i, j: (i, 0, 0)),
            pl.BlockSpec((1, d), lambda i, j: (0, 0)),
            _resident(w.shape, lambda i, j: (0, 0)),
        ] + [pl.BlockSpec((2, 3, dil, tm // dil, LANES), lambda i, j: (0, 0, 0, j, 0)) for dil in dils] + c_in,
        out_specs=[pl.BlockSpec((1, dil, tm // dil, 3 * d), lambda i, j: (i, 0, j, 0)) for dil in dils] + c_out,
        out_shape=[jax.ShapeDtypeStruct((b, dil, s // dil, 3 * d), _BF16) for dil in dils] + c_shapes,
        scratch_shapes=[pltpu.VMEM((d // LANES, tm, LANES), _F32)] * 2,
        compiler_params=_compiler_params(2),
        name="qkv",
    )(x, mod, g, w, *tabs, *c_args)


def _band_bias(n_keys):
    i = np.arange(Q_BLOCK)[:, None]
    j = np.arange(n_keys)[None, :]
    one = np.stack([np.where(np.abs(off + i - j) <= HALF_WINDOW, 0.0, NEG_INF)
                    for off in (0, HALF_WINDOW, 2 * HALF_WINDOW)])
    return np.concatenate([one, one], axis=1).astype(np.float32)


def _pair_bias():
    band = _band_bias(Q_BLOCK)[0]
    masked = np.full_like(band, NEG_INF)
    return np.stack([np.concatenate([band, masked], axis=1), np.concatenate([masked, band], axis=1)])


ATTN_UNROLL = 16
ATTN_PAIRS = 2


def _attention_kernel(q0, k0, v0, q1, k1, v1, q2, k2, v2, bias_w_ref, bias_p_ref, *refs, seq, cast_pers):
    n_cast = len(cast_pers)
    cast_in, o_ref = refs[:n_cast], refs[n_cast]
    cast_out, (s4_ref, s1_ref) = refs[n_cast + 1:2 * n_cast + 1], refs[2 * n_cast + 1:]
    step = pl.program_id(0) * pl.num_programs(1) + pl.program_id(1)
    _cast_blocks(cast_in, cast_out, cast_pers, step, True)
    lane = lax.broadcasted_iota(jnp.int32, (Q_BLOCK, LANES), 1)
    head_a = lane < HEAD_DIM
    for pair in range(ATTN_PAIRS):
        if pair == 1:
            _cast_blocks(cast_in, cast_out, cast_pers, step, False)
        _attention_pair(q0, k0, v0, q1, k1, v1, q2, k2, v2, bias_w_ref, bias_p_ref, o_ref,
                        s4_ref.at[pair], s1_ref.at[pair], pl.ds(pair * LANES, LANES), head_a, seq)


def _attention_pair(q0, k0, v0, q1, k1, v1, q2, k2, v2, bias_w_ref, bias_p_ref, o_ref,
                    s4_ref, s1_ref, lanes, head_a, seq):
    d4 = DILATED_GROUPS[1][1]
    len4 = seq // d4

    def block_stats(q, k, v, bias):
        n_keys = k.shape[0]
        zero = jnp.zeros_like(q)
        q2 = jnp.concatenate([jnp.where(head_a, q, zero), jnp.where(head_a, zero, q)], axis=0)
        s = lax.dot_general(q2, k, (((1,), (1,)), ((), ())), preferred_element_type=_F32) + bias
        m = jnp.max(s, axis=-1, keepdims=True)
        p = jnp.exp2(s - m).astype(_BF16)
        v_ext = jnp.concatenate([v, jnp.ones((n_keys, LANES), _BF16)], axis=1)
        pv = jnp.dot(p, v_ext, preferred_element_type=_F32)
        m_pair = jnp.where(head_a, jnp.broadcast_to(m[:Q_BLOCK], (Q_BLOCK, LANES)),
                           jnp.broadcast_to(m[Q_BLOCK:], (Q_BLOCK, LANES)))
        l_pair = jnp.where(head_a, pv[:Q_BLOCK, LANES:], pv[Q_BLOCK:, LANES:])
        acc_pair = jnp.where(head_a, pv[:Q_BLOCK, :LANES], pv[Q_BLOCK:, :LANES])
        return m_pair, l_pair, acc_pair

    def merge(old, new):
        m = jnp.maximum(old[0], new[0])
        a_old = jnp.exp2(old[0] - m)
        a_new = jnp.exp2(new[0] - m)
        return m, a_old * old[1] + a_new * new[1], a_old * old[2] + a_new * new[2]

    def load_block(q_ref, k_ref, v_ref, r, j, length):
        n_keys = min(length, Q_BLOCK + 2 * HALF_WINDOW)
        m0 = pl.multiple_of(j * Q_BLOCK, Q_BLOCK)
        k_start = pl.multiple_of(jnp.clip(m0 - HALF_WINDOW, 0, length - n_keys), HALF_WINDOW)
        bias = bias_w_ref[(m0 - k_start) // HALF_WINDOW]
        return (q_ref[0, r, pl.ds(m0, Q_BLOCK), lanes], k_ref[0, r, pl.ds(k_start, n_keys), lanes],
                v_ref[0, r, pl.ds(k_start, n_keys), lanes], bias)

    def load_pair_block(q_ref, k_ref, v_ref, r):
        first = pl.multiple_of((r // 2) * 2, 2)
        keys = lambda ref: ref[0, pl.ds(first, 2), :, lanes].reshape(2 * Q_BLOCK, LANES)
        return q_ref[0, r, :, lanes], keys(k_ref), keys(v_ref), bias_p_ref[r % 2]

    def group16(idx, carry):
        a, c = idx // d4, idx % d4
        state = block_stats(*load_pair_block(q2, k2, v2, idx))
        rows = pl.ds(c * len4 + a, Q_BLOCK, stride=d4)
        for t in range(3):
            s4_ref[t, rows, :] = state[t]
        return carry

    def group4(idx, carry):
        c, j = idx // (len4 // Q_BLOCK), idx % (len4 // Q_BLOCK)
        new = block_stats(*load_block(q1, k1, v1, c, j, len4))
        src = pl.ds(pl.multiple_of(c * len4 + j * Q_BLOCK, Q_BLOCK), Q_BLOCK)
        state = merge([s4_ref[t, src, :] for t in range(3)], new)
        rows = pl.ds(j * Q_BLOCK * d4 + c, Q_BLOCK, stride=d4)
        for t in range(3):
            s1_ref[t, rows, :] = state[t]
        return carry

    def group1(j, carry):
        new = block_stats(*load_block(q0, k0, v0, 0, j, seq))
        rows = pl.ds(pl.multiple_of(j * Q_BLOCK, Q_BLOCK), Q_BLOCK)
        _, l, acc = merge([s1_ref[t, rows, :] for t in range(3)], new)
        o_ref[0, rows, lanes] = (acc / l).astype(o_ref.dtype)
        return carry

    n_blocks = seq // Q_BLOCK
    lax.fori_loop(0, n_blocks, group16, 0, unroll=ATTN_UNROLL)
    lax.fori_loop(0, n_blocks, group4, 0, unroll=ATTN_UNROLL)
    lax.fori_loop(0, n_blocks, group1, 0, unroll=ATTN_UNROLL)


def _attention(qkv_groups, seq, cast_weights):
    b = qkv_groups[0].shape[0]
    d = D_MODEL
    width = ATTN_PAIRS * LANES
    n_steps = d // width
    in_specs, args = [], []
    for qkv, (_, dil) in zip(qkv_groups, DILATED_GROUPS):
        for part in range(3):
            in_specs.append(pl.BlockSpec((1, dil, seq // dil, width),
                                         lambda i, p, part=part: (i, 0, 0, part * n_steps + p)))
            args.append(qkv)
    wide = Q_BLOCK + 2 * HALF_WINDOW
    assert seq // DILATED_GROUPS[2][1] == Q_BLOCK and seq // DILATED_GROUPS[1][1] >= wide
    in_specs += [pl.BlockSpec((3, 2 * Q_BLOCK, wide), lambda i, p: (0, 0, 0)),
                 pl.BlockSpec((2, 2 * Q_BLOCK, wide), lambda i, p: (0, 0, 0))]
    args += [jnp.asarray(_band_bias(wide)), jnp.asarray(_pair_bias())]
    c_in, c_out, c_shapes, c_args, c_pers = _cast_specs(cast_weights, n_steps, b * n_steps)
    return pl.pallas_call(
        functools.partial(_attention_kernel, seq=seq, cast_pers=c_pers),
        grid=(b, n_steps),
        in_specs=in_specs + c_in,
        out_specs=[pl.BlockSpec((1, seq, width), lambda i, p: (i, 0, p))] + c_out,
        out_shape=[jax.ShapeDtypeStruct((b, seq, d), _BF16)] + c_shapes,
        scratch_shapes=[pltpu.VMEM((ATTN_PAIRS, 3, seq, LANES), _F32),
                        pltpu.VMEM((ATTN_PAIRS, 3, seq, LANES), _F32)],
        compiler_params=_compiler_params(2),
        name="dilated_attention",
    )(*args, *c_args)


FFN_CHUNK = MXU_WIDTH


TAIL_ROWS = 1024
TAIL_SUB_ROWS = 256


def _ffn_norm(x1, mod_ref, g_ref, out_ref, h_ref, rows):
    d = D_MODEL
    out_ref[0, rows, :] = x1
    h_ref[rows, :] = _modulated_norm(x1, g_ref[...], mod_ref[:, 3 * d:4 * d], mod_ref[:, 4 * d:5 * d]).astype(_BF16)


def _ffn_matmuls(mod_ref, w_in_ref, w_out_ref, out_ref, h_ref, act_ref, rows, finish=None):
    d, f = D_MODEL, FFN_HIDDEN
    h = h_ref[rows, :]
    for c in range(f // FFN_CHUNK):
        lo = c * FFN_CHUNK
        gate = jnp.dot(h, w_in_ref[:, lo:lo + FFN_CHUNK], preferred_element_type=_F32)
        up = jnp.dot(h, w_in_ref[:, f + lo:f + lo + FFN_CHUNK], preferred_element_type=_F32)
        act_ref[rows, lo:lo + FFN_CHUNK] = (gate * jax.nn.sigmoid(gate) * up).astype(_BF16)
    y = jnp.dot(act_ref[rows, :], w_out_ref[...], preferred_element_type=_F32)
    res = out_ref[0, rows, :] + mod_ref[:, 5 * d:6 * d] * y
    out_ref[0, rows, :] = res if finish is None else finish(res)


def _staggered(n_groups, stage_a, stage_b):
    stage_a(0)
    for k in range(n_groups):
        if k + 1 < n_groups:
            stage_a(k + 1)
        stage_b(k)


def _tail_rows(sub):
    return slice(sub * TAIL_SUB_ROWS, (sub + 1) * TAIL_SUB_ROWS)


def _post_attention_kernel(x_ref, o_ref, mod_ref, w_o_ref, g_ref, w_in_ref, w_out_ref, out_ref, act_ref, h_ref):
    d = D_MODEL

    def stage_a(sub):
        rows = _tail_rows(sub)
        y = jnp.dot(o_ref[0, rows, :], w_o_ref[...], preferred_element_type=_F32)
        _ffn_norm(x_ref[0, rows, :] + mod_ref[:, 2 * d:3 * d] * y, mod_ref, g_ref, out_ref, h_ref, rows)

    def stage_b(sub):
        _ffn_matmuls(mod_ref, w_in_ref, w_out_ref, out_ref, h_ref, act_ref, _tail_rows(sub))

    _staggered(TAIL_ROWS // TAIL_SUB_ROWS, stage_a, stage_b)


def _post_attention(x, o, mod, w_o, g, w_in, w_out):
    b, s, d = x.shape
    f = FFN_HIDDEN
    tm = TAIL_ROWS
    return pl.pallas_call(
        _post_attention_kernel,
        grid=(b, s // tm),
        in_specs=[
            pl.BlockSpec((1, tm, d), lambda i, j: (i, j, 0)),
            pl.BlockSpec((1, tm, d), lambda i, j: (i, j, 0)),
            pl.BlockSpec((None, 1, 6 * d), lambda i, j: (i, 0, 0)),
            _resident((d, d), lambda i, j: (0, 0)),
            pl.BlockSpec((1, d), lambda i, j: (0, 0)),
            _resident((d, 2 * f), lambda i, j: (0, 0)),
            _resident((f, d), lambda i, j: (0, 0)),
        ],
        out_specs=pl.BlockSpec((1, tm, d), lambda i, j: (i, j, 0)),
        out_shape=jax.ShapeDtypeStruct((b, s, d), _F32),
        scratch_shapes=[pltpu.VMEM((tm, f), _BF16), pltpu.VMEM((tm, d), _BF16)],
        compiler_params=_compiler_params(2),
        name="post_attention",
    )(x, o, mod, w_o, g, w_in, w_out)


CONV_IN_ROWS = 1024
CONV_IN_SUB_ROWS = 256


def _conv_in_kernel(x_ref, mod_ref, g_ref, w_ref, gate_ref, cu_ref, h_ref):
    d = D_MODEL

    def group_rows(sub):
        return slice(sub * CONV_IN_SUB_ROWS, (sub + 1) * CONV_IN_SUB_ROWS)

    def stage_a(sub):
        rows = group_rows(sub)
        h_ref[rows, :] = _modulated_norm(
            x_ref[0, rows, :], g_ref[...], mod_ref[:, 0:d], mod_ref[:, d:2 * d]).astype(_BF16)

    def stage_b(sub):
        rows = group_rows(sub)
        h = h_ref[rows, :]
        for j in range(d // MXU_WIDTH):
            lo = j * MXU_WIDTH
            gate_ref[0, rows, lo:lo + MXU_WIDTH] = jnp.dot(
                h, w_ref[:, lo:lo + MXU_WIDTH], preferred_element_type=_F32).astype(_BF16)
            c_gate = jnp.dot(h, w_ref[:, d + lo:d + lo + MXU_WIDTH], preferred_element_type=_F32)
            u = jnp.dot(h, w_ref[:, 2 * d + lo:2 * d + lo + MXU_WIDTH], preferred_element_type=_F32)
            cu_ref[0, rows, lo:lo + MXU_WIDTH] = (c_gate * u).astype(_BF16)

    _staggered(CONV_IN_ROWS // CONV_IN_SUB_ROWS, stage_a, stage_b)


def _conv_in(x, mod, g, w):
    b, s, d = x.shape
    tm = CONV_IN_ROWS
    tile = pl.BlockSpec((1, tm, d), lambda i, j: (i, j, 0))
    return pl.pallas_call(
        _conv_in_kernel,
        grid=(b, s // tm),
        in_specs=[
            tile,
            pl.BlockSpec((None, 1, 6 * d), lambda i, j: (i, 0, 0)),
            pl.BlockSpec((1, d), lambda i, j: (0, 0)),
            _resident((d, 3 * d), lambda i, j: (0, 0)),
        ],
        out_specs=[tile, tile],
        out_shape=[jax.ShapeDtypeStruct((b, s, d), _BF16)] * 2,
        scratch_shapes=[pltpu.VMEM((tm, d), _BF16)],
        compiler_params=_compiler_params(2),
        name="conv_in",
    )(x, mod, g, w)


HALO_ROWS = 16


def _conv_out_kernel(x_ref, gate_ref, cu_ref, prev_ref, next_ref, mod_ref, cw_ref, w_o_ref, g_ref,
                     w_in_ref, w_out_ref, gf_ref, out_ref, act_ref, h_ref, *, tm):
    d = D_MODEL
    j = pl.program_id(1)
    n_sub = tm // TAIL_SUB_ROWS
    row = lax.broadcasted_iota(jnp.int32, (TAIL_SUB_ROWS, d), 0)

    def stage_a(sub):
        lo, hi = sub * TAIL_SUB_ROWS, (sub + 1) * TAIL_SUB_ROWS
        rows = slice(lo, hi)
        cu = cu_ref[0, rows, :].astype(_F32)
        if sub == 0:
            prev_row = jnp.where(j > 0, prev_ref[0, HALO_ROWS - 1:HALO_ROWS, :].astype(_F32), 0.0)
        else:
            prev_row = cu_ref[0, lo - 1:lo, :].astype(_F32)
        if sub == n_sub - 1:
            next_row = jnp.where(j < pl.num_programs(1) - 1, next_ref[0, 0:1, :].astype(_F32), 0.0)
        else:
            next_row = cu_ref[0, hi:hi + 1, :].astype(_F32)
        before = jnp.where(row == 0, prev_row, pltpu.roll(cu, 1, 0))
        after = jnp.where(row == TAIL_SUB_ROWS - 1, next_row, pltpu.roll(cu, TAIL_SUB_ROWS - 1, 0))
        z = before * cw_ref[0:1, :] + cu * cw_ref[1:2, :] + after * cw_ref[2:3, :]
        gated = (gate_ref[0, rows, :].astype(_F32) * z).astype(_BF16)
        y = jnp.dot(gated, w_o_ref[...], preferred_element_type=_F32)
        _ffn_norm(x_ref[0, rows, :] + mod_ref[:, 2 * d:3 * d] * y, mod_ref, g_ref, out_ref, h_ref, rows)

    def stage_b(sub):
        _ffn_matmuls(mod_ref, w_in_ref, w_out_ref, out_ref, h_ref, act_ref, _tail_rows(sub),
                     finish=lambda x2: _rms_norm(x2, gf_ref[...]))

    _staggered(n_sub, stage_a, stage_b)


def _conv_out(x, gate, cu, mod, conv_w, w_o, g, w_in, w_out, final_g):
    b, s, d = x.shape
    f = FFN_HIDDEN
    tm = TAIL_ROWS
    per_tile = tm // HALO_ROWS
    n_halo = s // HALO_ROWS
    tile = pl.BlockSpec((1, tm, d), lambda i, j: (i, j, 0))
    return pl.pallas_call(
        functools.partial(_conv_out_kernel, tm=tm),
        grid=(b, s // tm),
        in_specs=[
            tile, tile, tile,
            pl.BlockSpec((1, HALO_ROWS, d), lambda i, j: (i, jnp.maximum(j * per_tile - 1, 0), 0)),
            pl.BlockSpec((1, HALO_ROWS, d), lambda i, j: (i, jnp.minimum((j + 1) * per_tile, n_halo - 1), 0)),
            pl.BlockSpec((None, 1, 6 * d), lambda i, j: (i, 0, 0)),
            pl.BlockSpec((CONV_WIDTH, d), lambda i, j: (0, 0)),
            _resident((d, d), lambda i, j: (0, 0)),
            pl.BlockSpec((1, d), lambda i, j: (0, 0)),
            _resident((d, 2 * f), lambda i, j: (0, 0)),
            _resident((f, d), lambda i, j: (0, 0)),
            pl.BlockSpec((1, d), lambda i, j: (0, 0)),
        ],
        out_specs=tile,
        out_shape=jax.ShapeDtypeStruct((b, s, d), _F32),
        scratch_shapes=[pltpu.VMEM((tm, f), _BF16), pltpu.VMEM((tm, d), _BF16)],
        compiler_params=_compiler_params(2),
        name="conv_out",
    )(x, gate, cu, cu, cu, mod, conv_w, w_o, g, w_in, w_out, final_g)


def kernel(x, c, attn_w_qkv, attn_w_o, conv_w_in, conv_w, conv_w_out, ada_w, ada_b,
           norm_mix_g, norm_ffn_g, ffn_w_in, ffn_w_out, final_g):
    b, s, d = x.shape
    assert d == D_MODEL and attn_w_qkv.shape[0] == 1 and conv_w_in.shape[0] == 1 and ada_w.shape[0] == 2
    mod = _modulation(c, ada_w, ada_b).reshape(2, b, 1, 6 * d)

    *qkv, w_o, w_in0, w_out0 = _qkv(x, mod[0], norm_mix_g[0:1], attn_w_qkv[0].astype(_BF16),
                                    [(attn_w_o, 0), (ffn_w_in, 0), (ffn_w_out, 0)])
    o, w_ci, w_co, w_in1, w_out1 = _attention(
        qkv, s, [(conv_w_in, 0), (conv_w_out, 0), (ffn_w_in, 1), (ffn_w_out, 1)])
    x = _post_attention(x, o, mod[0], w_o, norm_ffn_g[0:1], w_in0, w_out0)

    gate, cu = _conv_in(x, mod[1], norm_mix_g[1:2], w_ci)
    return _conv_out(x, gate, cu, mod[1], conv_w[0], w_co, norm_ffn_g[1:2],
                     w_in1, w_out1, final_g.reshape(1, d))
```

```python
import functools

import numpy as np
import jax
import jax.numpy as jnp
from jax import lax
from jax.experimental import pallas as pl
from jax.experimental.pallas import tpu as pltpu

D_MODEL = 1024
HEAD_DIM = 64
N_HEADS = D_MODEL // HEAD_DIM
DILATED_GROUPS = ((128, 1), (512, 4), (2048, 16))
ROPE_THETA = 500000.0
ROPE_DIM = HEAD_DIM // 4
CONV_WIDTH = 3
FFN_HIDDEN = -(-8 * D_MODEL // (3 * 256)) * 256
NORM_EPS = 1e-6
NEG_INF = -1e30

LANES = 128
MXU_WIDTH = 256
HALF_WINDOW = 64
Q_BLOCK = 128
SCORE_SCALE = HEAD_DIM ** -0.5 * float(np.log2(np.e))
VMEM_LIMIT_BYTES = 56 * 1024 * 1024

assert all(w // (2 * d) == HALF_WINDOW for w, d in DILATED_GROUPS)

_F32 = jnp.float32
_BF16 = jnp.bfloat16


def _compiler_params(n_grid_dims):
    return pltpu.CompilerParams(
        dimension_semantics=("arbitrary",) * n_grid_dims,
        vmem_limit_bytes=VMEM_LIMIT_BYTES,
    )


def _resident(block_shape, index_map):
    return pl.BlockSpec(block_shape, index_map, pipeline_mode=pl.Buffered(1))


def _rms_norm(x, g):
    r = lax.rsqrt(jnp.mean(x * x, axis=-1, keepdims=True) + NORM_EPS)
    return (x * r) * g


def _modulated_norm(x, g, shift, scale):
    return _rms_norm(x, g) * (1.0 + scale) + shift


def _modulation_kernel(c_ref, w_ref, b_ref, o_ref):
    c = c_ref[...]
    cond = (c * jax.nn.sigmoid(c)).astype(_BF16)
    o_ref[...] = jnp.dot(cond, w_ref[...].astype(_BF16), preferred_element_type=_F32) + b_ref[...]


def _modulation(c, ada_w, ada_b):
    depth, d, n = ada_w.shape
    b = c.shape[0]
    tn = 1536
    return pl.pallas_call(
        _modulation_kernel,
        grid=(depth, n // tn),
        in_specs=[
            pl.BlockSpec((b, d), lambda i, j: (0, 0)),
            pl.BlockSpec((None, d, tn), lambda i, j: (i, 0, j)),
            pl.BlockSpec((None, 1, tn), lambda i, j: (i, 0, j)),
        ],
        out_specs=pl.BlockSpec((None, b, tn), lambda i, j: (i, 0, j)),
        out_shape=jax.ShapeDtypeStruct((depth, b, n), _F32),
        compiler_params=_compiler_params(2),
        name="modulation",
    )(c, ada_w, ada_b.reshape(depth, 1, n))


CAST_ROWS = 16


def _cast_specs(weights, n_inner, total_steps):
    in_specs, out_specs, out_shapes, args, pers = [], [], [], [], []
    for w, layer in weights:
        rows, cols = w.shape[1:]
        n_blocks = max(nb for nb in range(1, total_steps + 1)
                       if total_steps % nb == 0 and rows % (nb * CAST_ROWS) == 0)
        block_rows = rows // n_blocks
        block = lambda i, j, per=total_steps // n_blocks: (i * n_inner + j) // per
        in_specs.append(pl.BlockSpec((None, block_rows, cols),
                                     lambda i, j, layer=layer, block=block: (layer, block(i, j), 0)))
        out_specs.append(pl.BlockSpec((block_rows, cols), lambda i, j, block=block: (block(i, j), 0)))
        out_shapes.append(jax.ShapeDtypeStruct((rows, cols), _BF16))
        args.append(w)
        pers.append(total_steps // n_blocks)
    return in_specs, out_specs, out_shapes, args, tuple(pers)


CAST_GUARD_STEPS = 4


def _cast_blocks(in_refs, out_refs, pers, step, guarded):
    for w_ref, o_ref, per in zip(in_refs, out_refs, pers):
        if guarded and per >= CAST_GUARD_STEPS:
            @pl.when(step % per == 0)
            def _(w_ref=w_ref, o_ref=o_ref):
                o_ref[...] = w_ref[...].astype(_BF16)
        elif not guarded and per < CAST_GUARD_STEPS:
            o_ref[...] = w_ref[...].astype(_BF16)


def _rope_tables(seq, dilation):
    half = ROPE_DIM // 2
    length = seq // dilation
    inv = ROPE_THETA ** (-np.arange(half, dtype=np.float64) * (2.0 / ROPE_DIM))
    pos = (np.arange(length)[None, :] * dilation + np.arange(dilation)[:, None]).astype(np.float64)
    ang = pos[:, :, None] * inv[None, None, :]
    cos, sin = np.cos(ang), np.sin(ang)
    c = np.ones((dilation, length, HEAD_DIM))
    sa = np.zeros((dilation, length, HEAD_DIM))
    sb = np.zeros((dilation, length, HEAD_DIM))
    c[:, :, :half] = cos
    c[:, :, half:ROPE_DIM] = cos
    sa[:, :, :half] = -sin
    sb[:, :, half:ROPE_DIM] = sin
    return np.stack([np.tile(t, (1, 1, LANES // HEAD_DIM)) for t in (c, sa, sb)]).astype(np.float32)


QKV_ROWS = 512
QKV_SUB_ROWS = 256


def _staggered(n_groups, stage_a, stage_b):
    stage_a(0)
    for k in range(n_groups):
        if k + 1 < n_groups:
            stage_a(k + 1)
        stage_b(k)


def _qkv_kernel(x_ref, mod_ref, g_ref, w_ref, tab1_ref, tab4_ref, tab16_ref, o1_ref, o4_ref, o16_ref,
                lhs_ref, hs1_ref, hs4_ref):
    d, tm = D_MODEL, QKV_SUB_ROWS
    d4 = DILATED_GROUPS[1][1]
    n_slabs = d // LANES

    def stage_a(sub):
        x = x_ref[0, sub * tm:(sub + 1) * tm, :]
        h = _modulated_norm(x, g_ref[...], mod_ref[:, 0:d], mod_ref[:, d:2 * d])
        for c in range(n_slabs):
            hs1_ref[c] = h[:, c * LANES:(c + 1) * LANES]
        rows4 = tm // d4
        for c in range(n_slabs):
            hs4_ref[c] = jnp.concatenate([hs1_ref[c, pl.ds(r, rows4, stride=d4), :] for r in range(d4)], axis=0)
        h4 = jnp.concatenate([hs4_ref[c] for c in range(n_slabs)], axis=1)
        rows16 = rows4 // d4
        h16 = jnp.concatenate(
            [jnp.concatenate([hs4_ref[c, pl.ds((r16 % d4) * rows4 + r16 // d4, rows16, stride=d4), :]
                              for c in range(n_slabs)], axis=1)
             for r16 in range(d4 * d4)], axis=0)
        for g, hg in enumerate((h, h4, h16)):
            lhs_ref[sub, g] = hg.astype(_BF16)

    def stage_b(sub):
        for g, (tab_ref, o_ref) in enumerate(((tab1_ref, o1_ref), (tab4_ref, o4_ref), (tab16_ref, o16_ref))):
            dilation = DILATED_GROUPS[g][1]
            rows = tm // dilation
            out_rows = slice(sub * rows, (sub + 1) * rows)
            hb = lhs_ref[sub, g]
            for part in range(3):
                for j in range(d // MXU_WIDTH):
                    col = part * d + j * MXU_WIDTH
                    res = jnp.dot(hb, w_ref[:, g * 3 * d + col:g * 3 * d + col + MXU_WIDTH],
                                  preferred_element_type=_F32)
                    if part < 2:
                        halves = []
                        for s in range(MXU_WIDTH // LANES):
                            xs = res[:, s * LANES:(s + 1) * LANES]
                            tabs = [jnp.concatenate([tab_ref[t, r, out_rows, :] for r in range(dilation)], axis=0)
                                    for t in range(3)]
                            halves.append(xs * tabs[0]
                                          + pltpu.roll(xs, LANES - ROPE_DIM // 2, 1) * tabs[1]
                                          + pltpu.roll(xs, ROPE_DIM // 2, 1) * tabs[2])
                        res = jnp.concatenate(halves, axis=1)
                        if part == 0:
                            res = res * SCORE_SCALE
                    res = res.astype(_BF16)
                    for r in range(dilation):
                        o_ref[0, r, out_rows, col:col + MXU_WIDTH] = res[r * rows:(r + 1) * rows]

    _staggered(QKV_ROWS // QKV_SUB_ROWS, stage_a, stage_b)


def _qkv(x, mod, g, w):
    b, s, d = x.shape
    tm = QKV_ROWS
    dils = [dil for _, dil in DILATED_GROUPS]
    assert dils == [1, 4, 16]
    tabs = [jnp.asarray(_rope_tables(s, dil)) for dil in dils]
    return pl.pallas_call(
        _qkv_kernel,
        grid=(b, s // tm),
        in_specs=[
            pl.BlockSpec((1, tm, d), lambda i, j: (i, j, 0)),
            pl.BlockSpec((None, 1, 6 * d), lambda i, j: (i, 0, 0)),
            pl.BlockSpec((1, d), lambda i, j: (0, 0)),
            _resident(w.shape, lambda i, j: (0, 0)),
        ] + [pl.BlockSpec((3, dil, tm // dil, LANES), lambda i, j: (0, 0, j, 0)) for dil in dils],
        out_specs=[pl.BlockSpec((1, dil, tm // dil, 3 * d), lambda i, j: (i, 0, j, 0)) for dil in dils],
        out_shape=[jax.ShapeDtypeStruct((b, dil, s // dil, 3 * d), _BF16) for dil in dils],
        scratch_shapes=[pltpu.VMEM((QKV_ROWS // QKV_SUB_ROWS, len(dils), QKV_SUB_ROWS, d), _BF16)]
        + [pltpu.VMEM((d // LANES, QKV_SUB_ROWS, LANES), _F32)] * 2,
        compiler_params=_compiler_params(2),
        name="qkv",
    )(x, mod, g, w, *tabs)


def _band_bias(n_keys):
    i = np.arange(Q_BLOCK)[:, None]
    j = np.arange(n_keys)[None, :]
    one = np.stack([np.where(np.abs(off + i - j) <= HALF_WINDOW, 0.0, NEG_INF)
                    for off in (0, HALF_WINDOW, 2 * HALF_WINDOW)])
    return np.concatenate([one, one], axis=1).astype(np.float32)


def _pair_bias():
    band = _band_bias(Q_BLOCK)[0]
    masked = np.full_like(band, NEG_INF)
    return np.stack([np.concatenate([band, masked], axis=1), np.concatenate([masked, band], axis=1)])


ATTN_UNROLL = 16
ATTN_PAIRS = 2


def _attention_kernel(q0, k0, v0, q1, k1, v1, q2, k2, v2, bias_w_ref, bias_p_ref, *refs, seq, cast_pers):
    n_cast = len(cast_pers)
    cast_in, o_ref = refs[:n_cast], refs[n_cast]
    cast_out, (s4_ref, s1_ref) = refs[n_cast + 1:2 * n_cast + 1], refs[2 * n_cast + 1:]
    step = pl.program_id(0) * pl.num_programs(1) + pl.program_id(1)
    _cast_blocks(cast_in, cast_out, cast_pers, step, True)
    lane = lax.broadcasted_iota(jnp.int32, (Q_BLOCK, LANES), 1)
    head_a = lane < HEAD_DIM
    for pair in range(ATTN_PAIRS):
        if pair == 1:
            _cast_blocks(cast_in, cast_out, cast_pers, step, False)
        _attention_pair(q0, k0, v0, q1, k1, v1, q2, k2, v2, bias_w_ref, bias_p_ref, o_ref,
                        s4_ref.at[pair], s1_ref.at[pair], pl.ds(pair * LANES, LANES), head_a, seq)


def _attention_pair(q0, k0, v0, q1, k1, v1, q2, k2, v2, bias_w_ref, bias_p_ref, o_ref,
                    s4_ref, s1_ref, lanes, head_a, seq):
    d4 = DILATED_GROUPS[1][1]
    len4 = seq // d4

    def block_stats(q, k, v, bias):
        n_keys = k.shape[0]
        zero = jnp.zeros_like(q)
        q2 = jnp.concatenate([jnp.where(head_a, q, zero), jnp.where(head_a, zero, q)], axis=0)
        s = lax.dot_general(q2, k, (((1,), (1,)), ((), ())), preferred_element_type=_F32) + bias
        m = jnp.max(s, axis=-1, keepdims=True)
        p = jnp.exp2(s - m).astype(_BF16)
        v_ext = jnp.concatenate([v, jnp.ones((n_keys, LANES), _BF16)], axis=1)
        pv = jnp.dot(p, v_ext, preferred_element_type=_F32)
        m_pair = jnp.where(head_a, jnp.broadcast_to(m[:Q_BLOCK], (Q_BLOCK, LANES)),
                           jnp.broadcast_to(m[Q_BLOCK:], (Q_BLOCK, LANES)))
        l_pair = jnp.where(head_a, pv[:Q_BLOCK, LANES:], pv[Q_BLOCK:, LANES:])
        acc_pair = jnp.where(head_a, pv[:Q_BLOCK, :LANES], pv[Q_BLOCK:, :LANES])
        return m_pair, l_pair, acc_pair

    def merge(old, new):
        m = jnp.maximum(old[0], new[0])
        a_old = jnp.exp2(old[0] - m)
        a_new = jnp.exp2(new[0] - m)
        return m, a_old * old[1] + a_new * new[1], a_old * old[2] + a_new * new[2]

    def load_block(q_ref, k_ref, v_ref, r, j, length):
        n_keys = min(length, Q_BLOCK + 2 * HALF_WINDOW)
        m0 = pl.multiple_of(j * Q_BLOCK, Q_BLOCK)
        k_start = pl.multiple_of(jnp.clip(m0 - HALF_WINDOW, 0, length - n_keys), HALF_WINDOW)
        bias = bias_w_ref[(m0 - k_start) // HALF_WINDOW]
        return (q_ref[0, r, pl.ds(m0, Q_BLOCK), lanes], k_ref[0, r, pl.ds(k_start, n_keys), lanes],
                v_ref[0, r, pl.ds(k_start, n_keys), lanes], bias)

    def load_pair_block(q_ref, k_ref, v_ref, r):
        first = pl.multiple_of((r // 2) * 2, 2)
        keys = lambda ref: ref[0, pl.ds(first, 2), :, lanes].reshape(2 * Q_BLOCK, LANES)
        return q_ref[0, r, :, lanes], keys(k_ref), keys(v_ref), bias_p_ref[r % 2]

    def group16(idx, carry):
        a, c = idx // d4, idx % d4
        state = block_stats(*load_pair_block(q2, k2, v2, idx))
        rows = pl.ds(c * len4 + a, Q_BLOCK, stride=d4)
        for t in range(3):
            s4_ref[t, rows, :] = state[t]
        return carry

    def group4(idx, carry):
        c, j = idx // (len4 // Q_BLOCK), idx % (len4 // Q_BLOCK)
        new = block_stats(*load_block(q1, k1, v1, c, j, len4))
        src = pl.ds(pl.multiple_of(c * len4 + j * Q_BLOCK, Q_BLOCK), Q_BLOCK)
        state = merge([s4_ref[t, src, :] for t in range(3)], new)
        rows = pl.ds(j * Q_BLOCK * d4 + c, Q_BLOCK, stride=d4)
        for t in range(3):
            s1_ref[t, rows, :] = state[t]
        return carry

    def group1(j, carry):
        new = block_stats(*load_block(q0, k0, v0, 0, j, seq))
        rows = pl.ds(pl.multiple_of(j * Q_BLOCK, Q_BLOCK), Q_BLOCK)
        _, l, acc = merge([s1_ref[t, rows, :] for t in range(3)], new)
        o_ref[0, rows, lanes] = (acc / l).astype(o_ref.dtype)
        return carry

    n_blocks = seq // Q_BLOCK
    lax.fori_loop(0, n_blocks, group16, 0, unroll=ATTN_UNROLL)
    lax.fori_loop(0, n_blocks, group4, 0, unroll=ATTN_UNROLL)
    lax.fori_loop(0, n_blocks, group1, 0, unroll=ATTN_UNROLL)


def _attention(qkv_groups, seq, cast_weights):
    b = qkv_groups[0].shape[0]
    d = D_MODEL
    width = ATTN_PAIRS * LANES
    n_steps = d // width
    in_specs, args = [], []
    for qkv, (_, dil) in zip(qkv_groups, DILATED_GROUPS):
        for part in range(3):
            in_specs.append(pl.BlockSpec((1, dil, seq // dil, width),
                                         lambda i, p, part=part: (i, 0, 0, part * n_steps + p)))
            args.append(qkv)
    wide = Q_BLOCK + 2 * HALF_WINDOW
    assert seq // DILATED_GROUPS[2][1] == Q_BLOCK and seq // DILATED_GROUPS[1][1] >= wide
    in_specs += [pl.BlockSpec((3, 2 * Q_BLOCK, wide), lambda i, p: (0, 0, 0)),
                 pl.BlockSpec((2, 2 * Q_BLOCK, wide), lambda i, p: (0, 0, 0))]
    args += [jnp.asarray(_band_bias(wide)), jnp.asarray(_pair_bias())]
    c_in, c_out, c_shapes, c_args, c_pers = _cast_specs(cast_weights, n_steps, b * n_steps)
    return pl.pallas_call(
        functools.partial(_attention_kernel, seq=seq, cast_pers=c_pers),
        grid=(b, n_steps),
        in_specs=in_specs + c_in,
        out_specs=[pl.BlockSpec((1, seq, width), lambda i, p: (i, 0, p))] + c_out,
        out_shape=[jax.ShapeDtypeStruct((b, seq, d), _BF16)] + c_shapes,
        scratch_shapes=[pltpu.VMEM((ATTN_PAIRS, 3, seq, LANES), _F32),
                        pltpu.VMEM((ATTN_PAIRS, 3, seq, LANES), _F32)],
        compiler_params=_compiler_params(2),
        name="dilated_attention",
    )(*args, *c_args)


FFN_CHUNK = MXU_WIDTH


TAIL_ROWS = 1024
TAIL_SUB_ROWS = 256


def _ffn_norm(x1, mod_ref, g_ref, out_ref, h_ref, rows):
    d = D_MODEL
    out_ref[0, rows, :] = x1
    h_ref[rows, :] = _modulated_norm(x1, g_ref[...], mod_ref[:, 3 * d:4 * d], mod_ref[:, 4 * d:5 * d]).astype(_BF16)


def _ffn_matmuls(mod_ref, w_in_ref, w_out_ref, out_ref, h_ref, act_ref, rows, finish=None):
    d, f = D_MODEL, FFN_HIDDEN
    h = h_ref[rows, :]
    for c in range(f // FFN_CHUNK):
        lo = c * FFN_CHUNK
        gate = jnp.dot(h, w_in_ref[:, lo:lo + FFN_CHUNK], preferred_element_type=_F32)
        up = jnp.dot(h, w_in_ref[:, f + lo:f + lo + FFN_CHUNK], preferred_element_type=_F32)
        act_ref[rows, lo:lo + FFN_CHUNK] = (gate * jax.nn.sigmoid(gate) * up).astype(_BF16)
    y = jnp.dot(act_ref[rows, :], w_out_ref[...], preferred_element_type=_F32)
    res = out_ref[0, rows, :] + mod_ref[:, 5 * d:6 * d] * y
    out_ref[0, rows, :] = res if finish is None else finish(res)


def _tail_rows(sub):
    return slice(sub * TAIL_SUB_ROWS, (sub + 1) * TAIL_SUB_ROWS)


def _post_attention_kernel(x_ref, o_ref, mod_ref, w_o_ref, g_ref, w_in_ref, w_out_ref, out_ref, act_ref, h_ref):
    d = D_MODEL

    def stage_a(sub):
        rows = _tail_rows(sub)
        y = jnp.dot(o_ref[0, rows, :], w_o_ref[...], preferred_element_type=_F32)
        _ffn_norm(x_ref[0, rows, :] + mod_ref[:, 2 * d:3 * d] * y, mod_ref, g_ref, out_ref, h_ref, rows)

    def stage_b(sub):
        _ffn_matmuls(mod_ref, w_in_ref, w_out_ref, out_ref, h_ref, act_ref, _tail_rows(sub))

    _staggered(TAIL_ROWS // TAIL_SUB_ROWS, stage_a, stage_b)


def _post_attention(x, o, mod, w_o, g, w_in, w_out):
    b, s, d = x.shape
    f = FFN_HIDDEN
    tm = TAIL_ROWS
    return pl.pallas_call(
        _post_attention_kernel,
        grid=(b, s // tm),
        in_specs=[
            pl.BlockSpec((1, tm, d), lambda i, j: (i, j, 0)),
            pl.BlockSpec((1, tm, d), lambda i, j: (i, j, 0)),
            pl.BlockSpec((None, 1, 6 * d), lambda i, j: (i, 0, 0)),
            _resident((d, d), lambda i, j: (0, 0)),
            pl.BlockSpec((1, d), lambda i, j: (0, 0)),
            _resident((d, 2 * f), lambda i, j: (0, 0)),
            _resident((f, d), lambda i, j: (0, 0)),
        ],
        out_specs=pl.BlockSpec((1, tm, d), lambda i, j: (i, j, 0)),
        out_shape=jax.ShapeDtypeStruct((b, s, d), _F32),
        scratch_shapes=[pltpu.VMEM((tm, f), _BF16), pltpu.VMEM((tm, d), _BF16)],
        compiler_params=_compiler_params(2),
        name="post_attention",
    )(x, o, mod, w_o, g, w_in, w_out)


CONV_IN_ROWS = 1024
CONV_IN_SUB_ROWS = 256


def _conv_in_kernel(x_ref, mod_ref, g_ref, w_ref, gate_ref, cu_ref, h_ref):
    d = D_MODEL

    def group_rows(sub):
        return slice(sub * CONV_IN_SUB_ROWS, (sub + 1) * CONV_IN_SUB_ROWS)

    def stage_a(sub):
        rows = group_rows(sub)
        h_ref[rows, :] = _modulated_norm(
            x_ref[0, rows, :], g_ref[...], mod_ref[:, 0:d], mod_ref[:, d:2 * d]).astype(_BF16)

    def stage_b(sub):
        rows = group_rows(sub)
        h = h_ref[rows, :]
        for j in range(d // MXU_WIDTH):
            lo = j * MXU_WIDTH
            gate_ref[0, rows, lo:lo + MXU_WIDTH] = jnp.dot(
                h, w_ref[:, lo:lo + MXU_WIDTH], preferred_element_type=_F32).astype(_BF16)
            c_gate = jnp.dot(h, w_ref[:, d + lo:d + lo + MXU_WIDTH], preferred_element_type=_F32)
            u = jnp.dot(h, w_ref[:, 2 * d + lo:2 * d + lo + MXU_WIDTH], preferred_element_type=_F32)
            cu_ref[0, rows, lo:lo + MXU_WIDTH] = (c_gate * u).astype(_BF16)

    _staggered(CONV_IN_ROWS // CONV_IN_SUB_ROWS, stage_a, stage_b)


def _conv_in(x, mod, g, w):
    b, s, d = x.shape
    tm = CONV_IN_ROWS
    tile = pl.BlockSpec((1, tm, d), lambda i, j: (i, j, 0))
    return pl.pallas_call(
        _conv_in_kernel,
        grid=(b, s // tm),
        in_specs=[
            tile,
            pl.BlockSpec((None, 1, 6 * d), lambda i, j: (i, 0, 0)),
            pl.BlockSpec((1, d), lambda i, j: (0, 0)),
            _resident((d, 3 * d), lambda i, j: (0, 0)),
        ],
        out_specs=[tile, tile],
        out_shape=[jax.ShapeDtypeStruct((b, s, d), _BF16)] * 2,
        scratch_shapes=[pltpu.VMEM((tm, d), _BF16)],
        compiler_params=_compiler_params(2),
        name="conv_in",
    )(x, mod, g, w)


HALO_ROWS = 16


def _conv_out_kernel(x_ref, gate_ref, cu_ref, prev_ref, next_ref, mod_ref, cw_ref, w_o_ref, g_ref,
                     w_in_ref, w_out_ref, gf_ref, out_ref, act_ref, h_ref, *, tm):
    d = D_MODEL
    j = pl.program_id(1)
    n_sub = tm // TAIL_SUB_ROWS
    row = lax.broadcasted_iota(jnp.int32, (TAIL_SUB_ROWS, d), 0)

    def stage_a(sub):
        lo, hi = sub * TAIL_SUB_ROWS, (sub + 1) * TAIL_SUB_ROWS
        rows = slice(lo, hi)
        cu = cu_ref[0, rows, :].astype(_F32)
        if sub == 0:
            prev_row = jnp.where(j > 0, prev_ref[0, HALO_ROWS - 1:HALO_ROWS, :].astype(_F32), 0.0)
        else:
            prev_row = cu_ref[0, lo - 1:lo, :].astype(_F32)
        if sub == n_sub - 1:
            next_row = jnp.where(j < pl.num_programs(1) - 1, next_ref[0, 0:1, :].astype(_F32), 0.0)
        else:
            next_row = cu_ref[0, hi:hi + 1, :].astype(_F32)
        before = jnp.where(row == 0, prev_row, pltpu.roll(cu, 1, 0))
        after = jnp.where(row == TAIL_SUB_ROWS - 1, next_row, pltpu.roll(cu, TAIL_SUB_ROWS - 1, 0))
        z = before * cw_ref[0:1, :] + cu * cw_ref[1:2, :] + after * cw_ref[2:3, :]
        gated = (gate_ref[0, rows, :].astype(_F32) * z).astype(_BF16)
        y = jnp.dot(gated, w_o_ref[...], preferred_element_type=_F32)
        _ffn_norm(x_ref[0, rows, :] + mod_ref[:, 2 * d:3 * d] * y, mod_ref, g_ref, out_ref, h_ref, rows)

    def stage_b(sub):
        _ffn_matmuls(mod_ref, w_in_ref, w_out_ref, out_ref, h_ref, act_ref, _tail_rows(sub),
                     finish=lambda x2: _rms_norm(x2, gf_ref[...]))

    _staggered(n_sub, stage_a, stage_b)


def _conv_out(x, gate, cu, mod, conv_w, w_o, g, w_in, w_out, final_g):
    b, s, d = x.shape
    f = FFN_HIDDEN
    tm = TAIL_ROWS
    per_tile = tm // HALO_ROWS
    n_halo = s // HALO_ROWS
    tile = pl.BlockSpec((1, tm, d), lambda i, j: (i, j, 0))
    return pl.pallas_call(
        functools.partial(_conv_out_kernel, tm=tm),
        grid=(b, s // tm),
        in_specs=[
            tile, tile, tile,
            pl.BlockSpec((1, HALO_ROWS, d), lambda i, j: (i, jnp.maximum(j * per_tile - 1, 0), 0)),
            pl.BlockSpec((1, HALO_ROWS, d), lambda i, j: (i, jnp.minimum((j + 1) * per_tile, n_halo - 1), 0)),
            pl.BlockSpec((None, 1, 6 * d), lambda i, j: (i, 0, 0)),
            pl.BlockSpec((CONV_WIDTH, d), lambda i, j: (0, 0)),
            _resident((d, d), lambda i, j: (0, 0)),
            pl.BlockSpec((1, d), lambda i, j: (0, 0)),
            _resident((d, 2 * f), lambda i, j: (0, 0)),
            _resident((f, d), lambda i, j: (0, 0)),
            pl.BlockSpec((1, d), lambda i, j: (0, 0)),
        ],
        out_specs=tile,
        out_shape=jax.ShapeDtypeStruct((b, s, d), _F32),
        scratch_shapes=[pltpu.VMEM((tm, f), _BF16), pltpu.VMEM((tm, d), _BF16)],
        compiler_params=_compiler_params(2),
        name="conv_out",
    )(x, gate, cu, cu, cu, mod, conv_w, w_o, g, w_in, w_out, final_g)


def kernel(x, c, attn_w_qkv, attn_w_o, conv_w_in, conv_w, conv_w_out, ada_w, ada_b,
           norm_mix_g, norm_ffn_g, ffn_w_in, ffn_w_out, final_g):
    b, s, d = x.shape
    assert d == D_MODEL and attn_w_qkv.shape[0] == 1 and conv_w_in.shape[0] == 1 and ada_w.shape[0] == 2
    mod = _modulation(c, ada_w, ada_b).reshape(2, b, 1, 6 * d)

    qkv = _qkv(x, mod[0], norm_mix_g[0:1], attn_w_qkv[0].astype(_BF16))
    o, w_o, w_in0, w_out0, w_ci, w_co, w_in1, w_out1 = _attention(
        qkv, s, [(attn_w_o, 0), (ffn_w_in, 0), (ffn_w_out, 0),
                 (conv_w_in, 0), (conv_w_out, 0), (ffn_w_in, 1), (ffn_w_out, 1)])
    x = _post_attention(x, o, mod[0], w_o, norm_ffn_g[0:1], w_in0, w_out0)

    gate, cu = _conv_in(x, mod[1], norm_mix_g[1:2], w_ci)
    return _conv_out(x, gate, cu, mod[1], conv_w[0], w_co, norm_ffn_g[1:2],
                     w_in1, w_out1, final_g.reshape(1, d))
```

```python
import functools

import numpy as np
import jax
import jax.numpy as jnp
from jax import lax
from jax.experimental import pallas as pl
from jax.experimental.pallas import tpu as pltpu

D_MODEL = 1024
HEAD_DIM = 64
N_HEADS = D_MODEL // HEAD_DIM
DILATED_GROUPS = ((128, 1), (512, 4), (2048, 16))
ROPE_THETA = 500000.0
ROPE_DIM = HEAD_DIM // 4
CONV_WIDTH = 3
FFN_HIDDEN = -(-8 * D_MODEL // (3 * 256)) * 256
NORM_EPS = 1e-6
NEG_INF = -1e30

LANES = 128
MXU_WIDTH = 256
HALF_WINDOW = 64
Q_BLOCK = 128
SCORE_SCALE = HEAD_DIM ** -0.5 * float(np.log2(np.e))
VMEM_LIMIT_BYTES = 56 * 1024 * 1024

assert all(w // (2 * d) == HALF_WINDOW for w, d in DILATED_GROUPS)

_F32 = jnp.float32
_BF16 = jnp.bfloat16


def _compiler_params(n_grid_dims):
    return pltpu.CompilerParams(
        dimension_semantics=("arbitrary",) * n_grid_dims,
        vmem_limit_bytes=VMEM_LIMIT_BYTES,
    )


def _resident(block_shape, index_map):
    return pl.BlockSpec(block_shape, index_map, pipeline_mode=pl.Buffered(1))


def _rms_norm(x, g):
    r = lax.rsqrt(jnp.mean(x * x, axis=-1, keepdims=True) + NORM_EPS)
    return (x * r) * g


def _modulated_norm(x, g, shift, scale):
    return _rms_norm(x, g) * (1.0 + scale) + shift


MODULATION_COLS = 1536


def _modulation_kernel(c_ref, w_ref, b_ref, o_ref):
    c = c_ref[...]
    cond = (c * jax.nn.sigmoid(c)).astype(_BF16)
    o_ref[...] = jnp.dot(cond, w_ref[...].astype(_BF16), preferred_element_type=_F32) + b_ref[...]


def _modulation(c, ada_w, ada_b):
    depth, d, n = ada_w.shape
    b = c.shape[0]
    tn = MODULATION_COLS
    return pl.pallas_call(
        _modulation_kernel,
        grid=(depth, n // tn),
        in_specs=[
            pl.BlockSpec((b, d), lambda i, j: (0, 0)),
            pl.BlockSpec((None, d, tn), lambda i, j: (i, 0, j)),
            pl.BlockSpec((None, 1, tn), lambda i, j: (i, 0, j)),
        ],
        out_specs=pl.BlockSpec((None, b, tn), lambda i, j: (i, 0, j)),
        out_shape=jax.ShapeDtypeStruct((depth, b, n), _F32),
        compiler_params=_compiler_params(2),
        name="modulation",
    )(c, ada_w, ada_b.reshape(depth, 1, n))


CAST_ROWS = 16


def _cast_specs(weights, n_inner, total_steps):
    in_specs, out_specs, out_shapes, args, pers = [], [], [], [], []
    for w, layer in weights:
        rows, cols = w.shape[1:]
        n_blocks = max(nb for nb in range(1, total_steps + 1)
                       if total_steps % nb == 0 and rows % (nb * CAST_ROWS) == 0)
        block_rows = rows // n_blocks
        block = lambda i, j, per=total_steps // n_blocks: (i * n_inner + j) // per
        in_specs.append(pl.BlockSpec((None, block_rows, cols),
                                     lambda i, j, layer=layer, block=block: (layer, block(i, j), 0)))
        out_specs.append(pl.BlockSpec((block_rows, cols), lambda i, j, block=block: (block(i, j), 0)))
        out_shapes.append(jax.ShapeDtypeStruct((rows, cols), _BF16))
        args.append(w)
        pers.append(total_steps // n_blocks)
    return in_specs, out_specs, out_shapes, args, tuple(pers)


CAST_GUARD_STEPS = 4


def _cast_blocks(in_refs, out_refs, pers, step, guarded):
    for w_ref, o_ref, per in zip(in_refs, out_refs, pers):
        if guarded and per >= CAST_GUARD_STEPS:
            @pl.when(step % per == 0)
            def _(w_ref=w_ref, o_ref=o_ref):
                o_ref[...] = w_ref[...].astype(_BF16)
        elif not guarded and per < CAST_GUARD_STEPS:
            o_ref[...] = w_ref[...].astype(_BF16)


def _rope_tables(seq, dilation):
    half = ROPE_DIM // 2
    length = seq // dilation
    inv = ROPE_THETA ** (-np.arange(half, dtype=np.float64) * (2.0 / ROPE_DIM))
    pos = (np.arange(length)[None, :] * dilation + np.arange(dilation)[:, None]).astype(np.float64)
    ang = pos[:, :, None] * inv[None, None, :]
    cos, sin = np.cos(ang), np.sin(ang)
    c = np.ones((dilation, length, HEAD_DIM))
    sa = np.zeros((dilation, length, HEAD_DIM))
    sb = np.zeros((dilation, length, HEAD_DIM))
    c[:, :, :half] = cos
    c[:, :, half:ROPE_DIM] = cos
    sa[:, :, :half] = -sin
    sb[:, :, half:ROPE_DIM] = sin
    return np.stack([np.tile(t, (1, 1, LANES // HEAD_DIM)) for t in (c, sa, sb)]).astype(np.float32)


QKV_ROWS = 512
QKV_SUB_ROWS = 256


def _staggered(n_groups, stage_a, stage_b):
    stage_a(0)
    for k in range(n_groups):
        if k + 1 < n_groups:
            stage_a(k + 1)
        stage_b(k)


def _qkv_kernel(x_ref, mod_ref, g_ref, w_ref, tab1_ref, tab4_ref, tab16_ref, o1_ref, o4_ref, o16_ref,
                lhs_ref, hs1_ref, hs4_ref):
    d, tm = D_MODEL, QKV_SUB_ROWS
    d4 = DILATED_GROUPS[1][1]
    n_slabs = d // LANES

    def stage_a(sub):
        x = x_ref[0, sub * tm:(sub + 1) * tm, :]
        h = _modulated_norm(x, g_ref[...], mod_ref[:, 0:d], mod_ref[:, d:2 * d])
        for c in range(n_slabs):
            hs1_ref[c] = h[:, c * LANES:(c + 1) * LANES]
        rows4 = tm // d4
        for c in range(n_slabs):
            hs4_ref[c] = jnp.concatenate([hs1_ref[c, pl.ds(r, rows4, stride=d4), :] for r in range(d4)], axis=0)
        h4 = jnp.concatenate([hs4_ref[c] for c in range(n_slabs)], axis=1)
        rows16 = rows4 // d4
        h16 = jnp.concatenate(
            [jnp.concatenate([hs4_ref[c, pl.ds((r16 % d4) * rows4 + r16 // d4, rows16, stride=d4), :]
                              for c in range(n_slabs)], axis=1)
             for r16 in range(d4 * d4)], axis=0)
        for g, hg in enumerate((h, h4, h16)):
            lhs_ref[sub, g] = hg.astype(_BF16)

    def stage_b(sub):
        for g, (tab_ref, o_ref) in enumerate(((tab1_ref, o1_ref), (tab4_ref, o4_ref), (tab16_ref, o16_ref))):
            dilation = DILATED_GROUPS[g][1]
            rows = tm // dilation
            out_rows = slice(sub * rows, (sub + 1) * rows)
            hb = lhs_ref[sub, g]
            for part in range(3):
                for j in range(d // MXU_WIDTH):
                    col = part * d + j * MXU_WIDTH
                    res = jnp.dot(hb, w_ref[:, g * 3 * d + col:g * 3 * d + col + MXU_WIDTH],
                                  preferred_element_type=_F32)
                    if part < 2:
                        halves = []
                        for s in range(MXU_WIDTH // LANES):
                            xs = res[:, s * LANES:(s + 1) * LANES]
                            tabs = [jnp.concatenate([tab_ref[t, r, out_rows, :] for r in range(dilation)], axis=0)
                                    for t in range(3)]
                            halves.append(xs * tabs[0]
                                          + pltpu.roll(xs, LANES - ROPE_DIM // 2, 1) * tabs[1]
                                          + pltpu.roll(xs, ROPE_DIM // 2, 1) * tabs[2])
                        res = jnp.concatenate(halves, axis=1)
                        if part == 0:
                            res = res * SCORE_SCALE
                    res = res.astype(_BF16)
                    for r in range(dilation):
                        o_ref[0, r, out_rows, col:col + MXU_WIDTH] = res[r * rows:(r + 1) * rows]

    _staggered(QKV_ROWS // QKV_SUB_ROWS, stage_a, stage_b)


def _qkv(x, mod, g, w):
    b, s, d = x.shape
    tm = QKV_ROWS
    dils = [dil for _, dil in DILATED_GROUPS]
    assert dils == [1, 4, 16]
    tabs = [jnp.asarray(_rope_tables(s, dil)) for dil in dils]
    return pl.pallas_call(
        _qkv_kernel,
        grid=(b, s // tm),
        in_specs=[
            pl.BlockSpec((1, tm, d), lambda i, j: (i, j, 0)),
            pl.BlockSpec((None, 1, 6 * d), lambda i, j: (i, 0, 0)),
            pl.BlockSpec((1, d), lambda i, j: (0, 0)),
            _resident(w.shape, lambda i, j: (0, 0)),
        ] + [pl.BlockSpec((3, dil, tm // dil, LANES), lambda i, j: (0, 0, j, 0)) for dil in dils],
        out_specs=[pl.BlockSpec((1, dil, tm // dil, 3 * d), lambda i, j: (i, 0, j, 0)) for dil in dils],
        out_shape=[jax.ShapeDtypeStruct((b, dil, s // dil, 3 * d), _BF16) for dil in dils],
        scratch_shapes=[pltpu.VMEM((QKV_ROWS // QKV_SUB_ROWS, len(dils), QKV_SUB_ROWS, d), _BF16)]
        + [pltpu.VMEM((d // LANES, QKV_SUB_ROWS, LANES), _F32)] * 2,
        compiler_params=_compiler_params(2),
        name="qkv",
    )(x, mod, g, w, *tabs)


def _band_bias(n_keys):
    i = np.arange(Q_BLOCK)[:, None]
    j = np.arange(n_keys)[None, :]
    one = np.stack([np.where(np.abs(off + i - j) <= HALF_WINDOW, 0.0, NEG_INF)
                    for off in (0, HALF_WINDOW, 2 * HALF_WINDOW)])
    return np.concatenate([one, one], axis=1).astype(np.float32)


def _pair_bias():
    band = _band_bias(Q_BLOCK)[0]
    masked = np.full_like(band, NEG_INF)
    return np.stack([np.concatenate([band, masked], axis=1), np.concatenate([masked, band], axis=1)])


ATTN_UNROLL = 16
ATTN_PAIRS = 2


def _attention_kernel(q0, k0, v0, q1, k1, v1, q2, k2, v2, bias_w_ref, bias_p_ref, *refs, seq, cast_pers):
    n_cast = len(cast_pers)
    cast_in, o_ref = refs[:n_cast], refs[n_cast]
    cast_out, (s4_ref, s1_ref) = refs[n_cast + 1:2 * n_cast + 1], refs[2 * n_cast + 1:]
    step = pl.program_id(0) * pl.num_programs(1) + pl.program_id(1)
    _cast_blocks(cast_in, cast_out, cast_pers, step, True)
    lane = lax.broadcasted_iota(jnp.int32, (Q_BLOCK, LANES), 1)
    head_a = lane < HEAD_DIM
    for pair in range(ATTN_PAIRS):
        if pair == 1:
            _cast_blocks(cast_in, cast_out, cast_pers, step, False)
        _attention_pair(q0, k0, v0, q1, k1, v1, q2, k2, v2, bias_w_ref, bias_p_ref, o_ref,
                        s4_ref.at[pair], s1_ref.at[pair], pl.ds(pair * LANES, LANES), head_a, seq)


def _attention_pair(q0, k0, v0, q1, k1, v1, q2, k2, v2, bias_w_ref, bias_p_ref, o_ref,
                    s4_ref, s1_ref, lanes, head_a, seq):
    d4 = DILATED_GROUPS[1][1]
    len4 = seq // d4

    def block_stats(q, k, v, bias):
        n_keys = k.shape[0]
        zero = jnp.zeros_like(q)
        q2 = jnp.concatenate([jnp.where(head_a, q, zero), jnp.where(head_a, zero, q)], axis=0)
        s = lax.dot_general(q2, k, (((1,), (1,)), ((), ())), preferred_element_type=_F32) + bias
        m = jnp.max(s, axis=-1, keepdims=True)
        p = jnp.exp2(s - m).astype(_BF16)
        v_ext = jnp.concatenate([v, jnp.ones((n_keys, LANES), _BF16)], axis=1)
        pv = jnp.dot(p, v_ext, preferred_element_type=_F32)
        m_pair = jnp.where(head_a, jnp.broadcast_to(m[:Q_BLOCK], (Q_BLOCK, LANES)),
                           jnp.broadcast_to(m[Q_BLOCK:], (Q_BLOCK, LANES)))
        l_pair = jnp.where(head_a, pv[:Q_BLOCK, LANES:], pv[Q_BLOCK:, LANES:])
        acc_pair = jnp.where(head_a, pv[:Q_BLOCK, :LANES], pv[Q_BLOCK:, :LANES])
        return m_pair, l_pair, acc_pair

    def merge(old, new):
        m = jnp.maximum(old[0], new[0])
        a_old = jnp.exp2(old[0] - m)
        a_new = jnp.exp2(new[0] - m)
        return m, a_old * old[1] + a_new * new[1], a_old * old[2] + a_new * new[2]

    def load_block(q_ref, k_ref, v_ref, r, j, length):
        n_keys = min(length, Q_BLOCK + 2 * HALF_WINDOW)
        m0 = pl.multiple_of(j * Q_BLOCK, Q_BLOCK)
        k_start = pl.multiple_of(jnp.clip(m0 - HALF_WINDOW, 0, length - n_keys), HALF_WINDOW)
        bias = bias_w_ref[(m0 - k_start) // HALF_WINDOW]
        return (q_ref[0, r, pl.ds(m0, Q_BLOCK), lanes], k_ref[0, r, pl.ds(k_start, n_keys), lanes],
                v_ref[0, r, pl.ds(k_start, n_keys), lanes], bias)

    def load_pair_block(q_ref, k_ref, v_ref, r):
        first = pl.multiple_of((r // 2) * 2, 2)
        keys = lambda ref: ref[0, pl.ds(first, 2), :, lanes].reshape(2 * Q_BLOCK, LANES)
        return q_ref[0, r, :, lanes], keys(k_ref), keys(v_ref), bias_p_ref[r % 2]

    def group16(idx, carry):
        a, c = idx // d4, idx % d4
        state = block_stats(*load_pair_block(q2, k2, v2, idx))
        rows = pl.ds(c * len4 + a, Q_BLOCK, stride=d4)
        for t in range(3):
            s4_ref[t, rows, :] = state[t]
        return carry

    def group4(idx, carry):
        c, j = idx // (len4 // Q_BLOCK), idx % (len4 // Q_BLOCK)
        new = block_stats(*load_block(q1, k1, v1, c, j, len4))
        src = pl.ds(pl.multiple_of(c * len4 + j * Q_BLOCK, Q_BLOCK), Q_BLOCK)
        state = merge([s4_ref[t, src, :] for t in range(3)], new)
        rows = pl.ds(j * Q_BLOCK * d4 + c, Q_BLOCK, stride=d4)
        for t in range(3):
            s1_ref[t, rows, :] = state[t]
        return carry

    def group1(j, carry):
        new = block_stats(*load_block(q0, k0, v0, 0, j, seq))
        rows = pl.ds(pl.multiple_of(j * Q_BLOCK, Q_BLOCK), Q_BLOCK)
        _, l, acc = merge([s1_ref[t, rows, :] for t in range(3)], new)
        o_ref[0, rows, lanes] = (acc / l).astype(o_ref.dtype)
        return carry

    n_blocks = seq // Q_BLOCK
    lax.fori_loop(0, n_blocks, group16, 0, unroll=ATTN_UNROLL)
    lax.fori_loop(0, n_blocks, group4, 0, unroll=ATTN_UNROLL)
    lax.fori_loop(0, n_blocks, group1, 0, unroll=ATTN_UNROLL)


def _attention(qkv_groups, seq, cast_weights):
    b = qkv_groups[0].shape[0]
    d = D_MODEL
    width = ATTN_PAIRS * LANES
    n_steps = d // width
    in_specs, args = [], []
    for qkv, (_, dil) in zip(qkv_groups, DILATED_GROUPS):
        for part in range(3):
            in_specs.append(pl.BlockSpec((1, dil, seq // dil, width),
                                         lambda i, p, part=part: (i, 0, 0, part * n_steps + p)))
            args.append(qkv)
    wide = Q_BLOCK + 2 * HALF_WINDOW
    assert seq // DILATED_GROUPS[2][1] == Q_BLOCK and seq // DILATED_GROUPS[1][1] >= wide
    in_specs += [pl.BlockSpec((3, 2 * Q_BLOCK, wide), lambda i, p: (0, 0, 0)),
                 pl.BlockSpec((2, 2 * Q_BLOCK, wide), lambda i, p: (0, 0, 0))]
    args += [jnp.asarray(_band_bias(wide)), jnp.asarray(_pair_bias())]
    c_in, c_out, c_shapes, c_args, c_pers = _cast_specs(cast_weights, n_steps, b * n_steps)
    return pl.pallas_call(
        functools.partial(_attention_kernel, seq=seq, cast_pers=c_pers),
        grid=(b, n_steps),
        in_specs=in_specs + c_in,
        out_specs=[pl.BlockSpec((1, seq, width), lambda i, p: (i, 0, p))] + c_out,
        out_shape=[jax.ShapeDtypeStruct((b, seq, d), _BF16)] + c_shapes,
        scratch_shapes=[pltpu.VMEM((ATTN_PAIRS, 3, seq, LANES), _F32),
                        pltpu.VMEM((ATTN_PAIRS, 3, seq, LANES), _F32)],
        compiler_params=_compiler_params(2),
        name="dilated_attention",
    )(*args, *c_args)


FFN_CHUNK = MXU_WIDTH


TAIL_ROWS = 1024
TAIL_SUB_ROWS = 256


def _ffn_norm(x1, mod_ref, g_ref, out_ref, h_ref, rows):
    d = D_MODEL
    out_ref[0, rows, :] = x1
    h_ref[rows, :] = _modulated_norm(x1, g_ref[...], mod_ref[:, 3 * d:4 * d], mod_ref[:, 4 * d:5 * d]).astype(_BF16)


def _ffn_matmuls(mod_ref, w_in_ref, w_out_ref, out_ref, h_ref, act_ref, rows, finish=None):
    d, f = D_MODEL, FFN_HIDDEN
    h = h_ref[rows, :]
    for c in range(f // FFN_CHUNK):
        lo = c * FFN_CHUNK
        gate = jnp.dot(h, w_in_ref[:, lo:lo + FFN_CHUNK], preferred_element_type=_F32)
        up = jnp.dot(h, w_in_ref[:, f + lo:f + lo + FFN_CHUNK], preferred_element_type=_F32)
        act_ref[rows, lo:lo + FFN_CHUNK] = (gate * jax.nn.sigmoid(gate) * up).astype(_BF16)
    y = jnp.dot(act_ref[rows, :], w_out_ref[...], preferred_element_type=_F32)
    res = out_ref[0, rows, :] + mod_ref[:, 5 * d:6 * d] * y
    out_ref[0, rows, :] = res if finish is None else finish(res)


def _tail_rows(sub):
    return slice(sub * TAIL_SUB_ROWS, (sub + 1) * TAIL_SUB_ROWS)


def _post_attention_kernel(x_ref, o_ref, mod_ref, w_o_ref, g_ref, w_in_ref, w_out_ref, out_ref, act_ref, h_ref):
    d = D_MODEL

    def stage_a(sub):
        rows = _tail_rows(sub)
        y = jnp.dot(o_ref[0, rows, :], w_o_ref[...], preferred_element_type=_F32)
        _ffn_norm(x_ref[0, rows, :] + mod_ref[:, 2 * d:3 * d] * y, mod_ref, g_ref, out_ref, h_ref, rows)

    def stage_b(sub):
        _ffn_matmuls(mod_ref, w_in_ref, w_out_ref, out_ref, h_ref, act_ref, _tail_rows(sub))

    _staggered(TAIL_ROWS // TAIL_SUB_ROWS, stage_a, stage_b)


def _post_attention(x, o, mod, w_o, g, w_in, w_out):
    b, s, d = x.shape
    f = FFN_HIDDEN
    tm = TAIL_ROWS
    return pl.pallas_call(
        _post_attention_kernel,
        grid=(b, s // tm),
        in_specs=[
            pl.BlockSpec((1, tm, d), lambda i, j: (i, j, 0)),
            pl.BlockSpec((1, tm, d), lambda i, j: (i, j, 0)),
            pl.BlockSpec((None, 1, 6 * d), lambda i, j: (i, 0, 0)),
            _resident((d, d), lambda i, j: (0, 0)),
            pl.BlockSpec((1, d), lambda i, j: (0, 0)),
            _resident((d, 2 * f), lambda i, j: (0, 0)),
            _resident((f, d), lambda i, j: (0, 0)),
        ],
        out_specs=pl.BlockSpec((1, tm, d), lambda i, j: (i, j, 0)),
        out_shape=jax.ShapeDtypeStruct((b, s, d), _F32),
        scratch_shapes=[pltpu.VMEM((tm, f), _BF16), pltpu.VMEM((tm, d), _BF16)],
        compiler_params=_compiler_params(2),
        name="post_attention",
    )(x, o, mod, w_o, g, w_in, w_out)


CONV_IN_ROWS = 1024
CONV_IN_SUB_ROWS = 256


def _conv_in_kernel(x_ref, mod_ref, g_ref, w_ref, gate_ref, cu_ref, h_ref):
    d = D_MODEL

    def group_rows(sub):
        return slice(sub * CONV_IN_SUB_ROWS, (sub + 1) * CONV_IN_SUB_ROWS)

    def stage_a(sub):
        rows = group_rows(sub)
        h_ref[rows, :] = _modulated_norm(
            x_ref[0, rows, :], g_ref[...], mod_ref[:, 0:d], mod_ref[:, d:2 * d]).astype(_BF16)

    def stage_b(sub):
        rows = group_rows(sub)
        h = h_ref[rows, :]
        for j in range(d // MXU_WIDTH):
            lo = j * MXU_WIDTH
            gate_ref[0, rows, lo:lo + MXU_WIDTH] = jnp.dot(
                h, w_ref[:, lo:lo + MXU_WIDTH], preferred_element_type=_F32).astype(_BF16)
            c_gate = jnp.dot(h, w_ref[:, d + lo:d + lo + MXU_WIDTH], preferred_element_type=_F32)
            u = jnp.dot(h, w_ref[:, 2 * d + lo:2 * d + lo + MXU_WIDTH], preferred_element_type=_F32)
            cu_ref[0, rows, lo:lo + MXU_WIDTH] = (c_gate * u).astype(_BF16)

    _staggered(CONV_IN_ROWS // CONV_IN_SUB_ROWS, stage_a, stage_b)


def _conv_in(x, mod, g, w):
    b, s, d = x.shape
    tm = CONV_IN_ROWS
    tile = pl.BlockSpec((1, tm, d), lambda i, j: (i, j, 0))
    return pl.pallas_call(
        _conv_in_kernel,
        grid=(b, s // tm),
        in_specs=[
            tile,
            pl.BlockSpec((None, 1, 6 * d), lambda i, j: (i, 0, 0)),
            pl.BlockSpec((1, d), lambda i, j: (0, 0)),
            _resident((d, 3 * d), lambda i, j: (0, 0)),
        ],
        out_specs=[tile, tile],
        out_shape=[jax.ShapeDtypeStruct((b, s, d), _BF16)] * 2,
        scratch_shapes=[pltpu.VMEM((tm, d), _BF16)],
        compiler_params=_compiler_params(2),
        name="conv_in",
    )(x, mod, g, w)


HALO_ROWS = 16


def _conv_out_kernel(x_ref, gate_ref, cu_ref, prev_ref, next_ref, mod_ref, cw_ref, w_o_ref, g_ref,
                     w_in_ref, w_out_ref, gf_ref, out_ref, act_ref, h_ref, *, tm):
    d = D_MODEL
    j = pl.program_id(1)
    n_sub = tm // TAIL_SUB_ROWS
    row = lax.broadcasted_iota(jnp.int32, (TAIL_SUB_ROWS, d), 0)

    def stage_a(sub):
        lo, hi = sub * TAIL_SUB_ROWS, (sub + 1) * TAIL_SUB_ROWS
        rows = slice(lo, hi)
        cu = cu_ref[0, rows, :].astype(_F32)
        if sub == 0:
            prev_row = jnp.where(j > 0, prev_ref[0, HALO_ROWS - 1:HALO_ROWS, :].astype(_F32), 0.0)
        else:
            prev_row = cu_ref[0, lo - 1:lo, :].astype(_F32)
        if sub == n_sub - 1:
            next_row = jnp.where(j < pl.num_programs(1) - 1, next_ref[0, 0:1, :].astype(_F32), 0.0)
        else:
            next_row = cu_ref[0, hi:hi + 1, :].astype(_F32)
        before = jnp.where(row == 0, prev_row, pltpu.roll(cu, 1, 0))
        after = jnp.where(row == TAIL_SUB_ROWS - 1, next_row, pltpu.roll(cu, TAIL_SUB_ROWS - 1, 0))
        z = before * cw_ref[0:1, :] + cu * cw_ref[1:2, :] + after * cw_ref[2:3, :]
        gated = (gate_ref[0, rows, :].astype(_F32) * z).astype(_BF16)
        y = jnp.dot(gated, w_o_ref[...], preferred_element_type=_F32)
        _ffn_norm(x_ref[0, rows, :] + mod_ref[:, 2 * d:3 * d] * y, mod_ref, g_ref, out_ref, h_ref, rows)

    def stage_b(sub):
        _ffn_matmuls(mod_ref, w_in_ref, w_out_ref, out_ref, h_ref, act_ref, _tail_rows(sub),
                     finish=lambda x2: _rms_norm(x2, gf_ref[...]))

    _staggered(n_sub, stage_a, stage_b)


def _conv_out(x, gate, cu, mod, conv_w, w_o, g, w_in, w_out, final_g):
    b, s, d = x.shape
    f = FFN_HIDDEN
    tm = TAIL_ROWS
    per_tile = tm // HALO_ROWS
    n_halo = s // HALO_ROWS
    tile = pl.BlockSpec((1, tm, d), lambda i, j: (i, j, 0))
    return pl.pallas_call(
        functools.partial(_conv_out_kernel, tm=tm),
        grid=(b, s // tm),
        in_specs=[
            tile, tile, tile,
            pl.BlockSpec((1, HALO_ROWS, d), lambda i, j: (i, jnp.maximum(j * per_tile - 1, 0), 0)),
            pl.BlockSpec((1, HALO_ROWS, d), lambda i, j: (i, jnp.minimum((j + 1) * per_tile, n_halo - 1), 0)),
            pl.BlockSpec((None, 1, 6 * d), lambda i, j: (i, 0, 0)),
            pl.BlockSpec((CONV_WIDTH, d), lambda i, j: (0, 0)),
            _resident((d, d), lambda i, j: (0, 0)),
            pl.BlockSpec((1, d), lambda i, j: (0, 0)),
            _resident((d, 2 * f), lambda i, j: (0, 0)),
            _resident((f, d), lambda i, j: (0, 0)),
            pl.BlockSpec((1, d), lambda i, j: (0, 0)),
        ],
        out_specs=tile,
        out_shape=jax.ShapeDtypeStruct((b, s, d), _F32),
        scratch_shapes=[pltpu.VMEM((tm, f), _BF16), pltpu.VMEM((tm, d), _BF16)],
        compiler_params=_compiler_params(2),
        name="conv_out",
    )(x, gate, cu, cu, cu, mod, conv_w, w_o, g, w_in, w_out, final_g)


def kernel(x, c, attn_w_qkv, attn_w_o, conv_w_in, conv_w, conv_w_out, ada_w, ada_b,
           norm_mix_g, norm_ffn_g, ffn_w_in, ffn_w_out, final_g):
    b, s, d = x.shape
    assert d == D_MODEL and attn_w_qkv.shape[0] == 1 and conv_w_in.shape[0] == 1 and ada_w.shape[0] == 2
    mod = _modulation(c, ada_w, ada_b).reshape(2, b, 1, 6 * d)

    qkv = _qkv(x, mod[0], norm_mix_g[0:1], attn_w_qkv[0].astype(_BF16))
    o, w_o, w_in0, w_out0, w_ci, w_co, w_in1, w_out1 = _attention(
        qkv, s, [(attn_w_o, 0), (ffn_w_in, 0), (ffn_w_out, 0),
                 (conv_w_in, 0), (conv_w_out, 0), (ffn_w_in, 1), (ffn_w_out, 1)])
    x = _post_attention(x, o, mod[0], w_o, norm_ffn_g[0:1], w_in0, w_out0)

    gate, cu = _conv_in(x, mod[1], norm_mix_g[1:2], w_ci)
    return _conv_out(x, gate, cu, mod[1], conv_w[0], w_co, norm_ffn_g[1:2],
                     w_in1, w_out1, final_g.reshape(1, d))
```

```python
import functools

import numpy as np
import jax
import jax.numpy as jnp
from jax import lax
from jax.experimental import pallas as pl
from jax.experimental.pallas import tpu as pltpu

D_MODEL = 1024
HEAD_DIM = 64
N_HEADS = D_MODEL // HEAD_DIM
DILATED_GROUPS = ((128, 1), (512, 4), (2048, 16))
ROPE_THETA = 500000.0
ROPE_DIM = HEAD_DIM // 4
CONV_WIDTH = 3
FFN_HIDDEN = -(-8 * D_MODEL // (3 * 256)) * 256
NORM_EPS = 1e-6
NEG_INF = -1e30

LANES = 128
MXU_WIDTH = 256
HALF_WINDOW = 64
Q_BLOCK = 128
SCORE_SCALE = HEAD_DIM ** -0.5 * float(np.log2(np.e))
VMEM_LIMIT_BYTES = 56 * 1024 * 1024

assert all(w // (2 * d) == HALF_WINDOW for w, d in DILATED_GROUPS)

_F32 = jnp.float32
_BF16 = jnp.bfloat16


def _compiler_params(n_grid_dims):
    return pltpu.CompilerParams(
        dimension_semantics=("arbitrary",) * n_grid_dims,
        vmem_limit_bytes=VMEM_LIMIT_BYTES,
    )


def _resident(block_shape, index_map):
    return pl.BlockSpec(block_shape, index_map, pipeline_mode=pl.Buffered(1))


def _rms_norm(x, g):
    r = lax.rsqrt(jnp.mean(x * x, axis=-1, keepdims=True) + NORM_EPS)
    return (x * r) * g


def _modulated_norm(x, g, shift, scale):
    return _rms_norm(x, g) * (1.0 + scale) + shift


MODULATION_COLS = 1536


def _modulation_kernel(c_ref, w_ref, b_ref, wq_ref, o_ref, wq_bf_ref):
    c = c_ref[...]
    cond = (c * jax.nn.sigmoid(c)).astype(_BF16)
    o_ref[...] = jnp.dot(cond, w_ref[...].astype(_BF16), preferred_element_type=_F32) + b_ref[...]
    wq_bf_ref[...] = wq_ref[...].astype(_BF16)


def _modulation(c, ada_w, ada_b, w_qkv):
    depth, d, n = ada_w.shape
    b = c.shape[0]
    tn = MODULATION_COLS
    n_j = n // tn
    rows, cols = w_qkv.shape[1:]
    wq_rows = rows // (depth * n_j)
    assert wq_rows % CAST_ROWS == 0
    return pl.pallas_call(
        _modulation_kernel,
        grid=(depth, n_j),
        in_specs=[
            pl.BlockSpec((b, d), lambda i, j: (0, 0)),
            pl.BlockSpec((None, d, tn), lambda i, j: (i, 0, j)),
            pl.BlockSpec((None, 1, tn), lambda i, j: (i, 0, j)),
            pl.BlockSpec((None, wq_rows, cols), lambda i, j: (0, i * n_j + j, 0)),
        ],
        out_specs=[pl.BlockSpec((None, b, tn), lambda i, j: (i, 0, j)),
                   pl.BlockSpec((wq_rows, cols), lambda i, j: (i * n_j + j, 0))],
        out_shape=[jax.ShapeDtypeStruct((depth, b, n), _F32), jax.ShapeDtypeStruct((rows, cols), _BF16)],
        compiler_params=_compiler_params(2),
        name="modulation",
    )(c, ada_w, ada_b.reshape(depth, 1, n), w_qkv)


CAST_ROWS = 16


def _cast_specs(weights, n_inner, total_steps):
    in_specs, out_specs, out_shapes, args, pers = [], [], [], [], []
    for w, layer in weights:
        rows, cols = w.shape[1:]
        n_blocks = max(nb for nb in range(1, total_steps + 1)
                       if total_steps % nb == 0 and rows % (nb * CAST_ROWS) == 0)
        block_rows = rows // n_blocks
        block = lambda i, j, per=total_steps // n_blocks: (i * n_inner + j) // per
        in_specs.append(pl.BlockSpec((None, block_rows, cols),
                                     lambda i, j, layer=layer, block=block: (layer, block(i, j), 0)))
        out_specs.append(pl.BlockSpec((block_rows, cols), lambda i, j, block=block: (block(i, j), 0)))
        out_shapes.append(jax.ShapeDtypeStruct((rows, cols), _BF16))
        args.append(w)
        pers.append(total_steps // n_blocks)
    return in_specs, out_specs, out_shapes, args, tuple(pers)


CAST_GUARD_STEPS = 4


def _cast_blocks(in_refs, out_refs, pers, step, guarded):
    for w_ref, o_ref, per in zip(in_refs, out_refs, pers):
        if guarded and per >= CAST_GUARD_STEPS:
            @pl.when(step % per == 0)
            def _(w_ref=w_ref, o_ref=o_ref):
                o_ref[...] = w_ref[...].astype(_BF16)
        elif not guarded and per < CAST_GUARD_STEPS:
            o_ref[...] = w_ref[...].astype(_BF16)


def _rope_tables(seq, dilation):
    half = ROPE_DIM // 2
    length = seq // dilation
    inv = ROPE_THETA ** (-np.arange(half, dtype=np.float64) * (2.0 / ROPE_DIM))
    pos = (np.arange(length)[None, :] * dilation + np.arange(dilation)[:, None]).astype(np.float64)
    ang = pos[:, :, None] * inv[None, None, :]
    cos, sin = np.cos(ang), np.sin(ang)
    c = np.ones((dilation, length, HEAD_DIM))
    sa = np.zeros((dilation, length, HEAD_DIM))
    sb = np.zeros((dilation, length, HEAD_DIM))
    c[:, :, :half] = cos
    c[:, :, half:ROPE_DIM] = cos
    sa[:, :, :half] = -sin
    sb[:, :, half:ROPE_DIM] = sin
    return np.stack([np.tile(t, (1, 1, LANES // HEAD_DIM)) for t in (c, sa, sb)]).astype(np.float32)


QKV_ROWS = 512
QKV_SUB_ROWS = 256


def _staggered(n_groups, stage_a, stage_b):
    stage_a(0)
    for k in range(n_groups):
        if k + 1 < n_groups:
            stage_a(k + 1)
        stage_b(k)


def _qkv_kernel(x_ref, mod_ref, g_ref, w_ref, tab1_ref, tab4_ref, tab16_ref, o1_ref, o4_ref, o16_ref,
                lhs_ref, hs1_ref, hs4_ref):
    d, tm = D_MODEL, QKV_SUB_ROWS
    d4 = DILATED_GROUPS[1][1]
    n_slabs = d // LANES

    def stage_a(sub):
        x = x_ref[0, sub * tm:(sub + 1) * tm, :]
        h = _modulated_norm(x, g_ref[...], mod_ref[:, 0:d], mod_ref[:, d:2 * d])
        for c in range(n_slabs):
            hs1_ref[c] = h[:, c * LANES:(c + 1) * LANES]
        rows4 = tm // d4
        for c in range(n_slabs):
            hs4_ref[c] = jnp.concatenate([hs1_ref[c, pl.ds(r, rows4, stride=d4), :] for r in range(d4)], axis=0)
        h4 = jnp.concatenate([hs4_ref[c] for c in range(n_slabs)], axis=1)
        rows16 = rows4 // d4
        h16 = jnp.concatenate(
            [jnp.concatenate([hs4_ref[c, pl.ds((r16 % d4) * rows4 + r16 // d4, rows16, stride=d4), :]
                              for c in range(n_slabs)], axis=1)
             for r16 in range(d4 * d4)], axis=0)
        for g, hg in enumerate((h, h4, h16)):
            lhs_ref[sub, g] = hg.astype(_BF16)

    def stage_b(sub):
        for g, (tab_ref, o_ref) in enumerate(((tab1_ref, o1_ref), (tab4_ref, o4_ref), (tab16_ref, o16_ref))):
            dilation = DILATED_GROUPS[g][1]
            rows = tm // dilation
            out_rows = slice(sub * rows, (sub + 1) * rows)
            hb = lhs_ref[sub, g]
            for part in range(3):
                for j in range(d // MXU_WIDTH):
                    col = part * d + j * MXU_WIDTH
                    res = jnp.dot(hb, w_ref[:, g * 3 * d + col:g * 3 * d + col + MXU_WIDTH],
                                  preferred_element_type=_F32)
                    if part < 2:
                        halves = []
                        for s in range(MXU_WIDTH // LANES):
                            xs = res[:, s * LANES:(s + 1) * LANES]
                            tabs = [jnp.concatenate([tab_ref[t, r, out_rows, :] for r in range(dilation)], axis=0)
                                    for t in range(3)]
                            halves.append(xs * tabs[0]
                                          + pltpu.roll(xs, LANES - ROPE_DIM // 2, 1) * tabs[1]
                                          + pltpu.roll(xs, ROPE_DIM // 2, 1) * tabs[2])
                        res = jnp.concatenate(halves, axis=1)
                        if part == 0:
                            res = res * SCORE_SCALE
                    res = res.astype(_BF16)
                    for r in range(dilation):
                        o_ref[0, r, out_rows, col:col + MXU_WIDTH] = res[r * rows:(r + 1) * rows]

    _staggered(QKV_ROWS // QKV_SUB_ROWS, stage_a, stage_b)


def _qkv(x, mod, g, w):
    b, s, d = x.shape
    tm = QKV_ROWS
    dils = [dil for _, dil in DILATED_GROUPS]
    assert dils == [1, 4, 16]
    tabs = [jnp.asarray(_rope_tables(s, dil)) for dil in dils]
    return pl.pallas_call(
        _qkv_kernel,
        grid=(b, s // tm),
        in_specs=[
            pl.BlockSpec((1, tm, d), lambda i, j: (i, j, 0)),
            pl.BlockSpec((None, 1, 6 * d), lambda i, j: (i, 0, 0)),
            pl.BlockSpec((1, d), lambda i, j: (0, 0)),
            _resident(w.shape, lambda i, j: (0, 0)),
        ] + [pl.BlockSpec((3, dil, tm // dil, LANES), lambda i, j: (0, 0, j, 0)) for dil in dils],
        out_specs=[pl.BlockSpec((1, dil, tm // dil, 3 * d), lambda i, j: (i, 0, j, 0)) for dil in dils],
        out_shape=[jax.ShapeDtypeStruct((b, dil, s // dil, 3 * d), _BF16) for dil in dils],
        scratch_shapes=[pltpu.VMEM((QKV_ROWS // QKV_SUB_ROWS, len(dils), QKV_SUB_ROWS, d), _BF16)]
        + [pltpu.VMEM((d // LANES, QKV_SUB_ROWS, LANES), _F32)] * 2,
        compiler_params=_compiler_params(2),
        name="qkv",
    )(x, mod, g, w, *tabs)


def _band_bias(n_keys):
    i = np.arange(Q_BLOCK)[:, None]
    j = np.arange(n_keys)[None, :]
    one = np.stack([np.where(np.abs(off + i - j) <= HALF_WINDOW, 0.0, NEG_INF)
                    for off in (0, HALF_WINDOW, 2 * HALF_WINDOW)])
    return np.concatenate([one, one], axis=1).astype(np.float32)


def _pair_bias():
    band = _band_bias(Q_BLOCK)[0]
    masked = np.full_like(band, NEG_INF)
    return np.stack([np.concatenate([band, masked], axis=1), np.concatenate([masked, band], axis=1)])


ATTN_UNROLL = 16
ATTN_PAIRS = 2


def _attention_kernel(q0, k0, v0, q1, k1, v1, q2, k2, v2, bias_w_ref, bias_p_ref, *refs, seq, cast_pers):
    n_cast = len(cast_pers)
    cast_in, o_ref = refs[:n_cast], refs[n_cast]
    cast_out, (s4_ref, s1_ref) = refs[n_cast + 1:2 * n_cast + 1], refs[2 * n_cast + 1:]
    step = pl.program_id(0) * pl.num_programs(1) + pl.program_id(1)
    _cast_blocks(cast_in, cast_out, cast_pers, step, True)
    lane = lax.broadcasted_iota(jnp.int32, (Q_BLOCK, LANES), 1)
    head_a = lane < HEAD_DIM
    for pair in range(ATTN_PAIRS):
        if pair == 1:
            _cast_blocks(cast_in, cast_out, cast_pers, step, False)
        _attention_pair(q0, k0, v0, q1, k1, v1, q2, k2, v2, bias_w_ref, bias_p_ref, o_ref,
                        s4_ref.at[pair], s1_ref.at[pair], pl.ds(pair * LANES, LANES), head_a, seq)


def _attention_pair(q0, k0, v0, q1, k1, v1, q2, k2, v2, bias_w_ref, bias_p_ref, o_ref,
                    s4_ref, s1_ref, lanes, head_a, seq):
    d4 = DILATED_GROUPS[1][1]
    len4 = seq // d4

    def block_stats(q, k, v, bias):
        n_keys = k.shape[0]
        zero = jnp.zeros_like(q)
        q2 = jnp.concatenate([jnp.where(head_a, q, zero), jnp.where(head_a, zero, q)], axis=0)
        s = lax.dot_general(q2, k, (((1,), (1,)), ((), ())), preferred_element_type=_F32) + bias
        m = jnp.max(s, axis=-1, keepdims=True)
        p = jnp.exp2(s - m).astype(_BF16)
        v_ext = jnp.concatenate([v, jnp.ones((n_keys, LANES), _BF16)], axis=1)
        pv = jnp.dot(p, v_ext, preferred_element_type=_F32)
        m_pair = jnp.where(head_a, jnp.broadcast_to(m[:Q_BLOCK], (Q_BLOCK, LANES)),
                           jnp.broadcast_to(m[Q_BLOCK:], (Q_BLOCK, LANES)))
        l_pair = jnp.where(head_a, pv[:Q_BLOCK, LANES:], pv[Q_BLOCK:, LANES:])
        acc_pair = jnp.where(head_a, pv[:Q_BLOCK, :LANES], pv[Q_BLOCK:, :LANES])
        return m_pair, l_pair, acc_pair

    def merge(old, new):
        m = jnp.maximum(old[0], new[0])
        a_old = jnp.exp2(old[0] - m)
        a_new = jnp.exp2(new[0] - m)
        return m, a_old * old[1] + a_new * new[1], a_old * old[2] + a_new * new[2]

    def load_block(q_ref, k_ref, v_ref, r, j, length):
        n_keys = min(length, Q_BLOCK + 2 * HALF_WINDOW)
        m0 = pl.multiple_of(j * Q_BLOCK, Q_BLOCK)
        k_start = pl.multiple_of(jnp.clip(m0 - HALF_WINDOW, 0, length - n_keys), HALF_WINDOW)
        bias = bias_w_ref[(m0 - k_start) // HALF_WINDOW]
        return (q_ref[0, r, pl.ds(m0, Q_BLOCK), lanes], k_ref[0, r, pl.ds(k_start, n_keys), lanes],
                v_ref[0, r, pl.ds(k_start, n_keys), lanes], bias)

    def load_pair_block(q_ref, k_ref, v_ref, r):
        first = pl.multiple_of((r // 2) * 2, 2)
        keys = lambda ref: ref[0, pl.ds(first, 2), :, lanes].reshape(2 * Q_BLOCK, LANES)
        return q_ref[0, r, :, lanes], keys(k_ref), keys(v_ref), bias_p_ref[r % 2]

    def group16(idx, carry):
        a, c = idx // d4, idx % d4
        state = block_stats(*load_pair_block(q2, k2, v2, idx))
        rows = pl.ds(c * len4 + a, Q_BLOCK, stride=d4)
        for t in range(3):
            s4_ref[t, rows, :] = state[t]
        return carry

    def group4(idx, carry):
        c, j = idx // (len4 // Q_BLOCK), idx % (len4 // Q_BLOCK)
        new = block_stats(*load_block(q1, k1, v1, c, j, len4))
        src = pl.ds(pl.multiple_of(c * len4 + j * Q_BLOCK, Q_BLOCK), Q_BLOCK)
        state = merge([s4_ref[t, src, :] for t in range(3)], new)
        rows = pl.ds(j * Q_BLOCK * d4 + c, Q_BLOCK, stride=d4)
        for t in range(3):
            s1_ref[t, rows, :] = state[t]
        return carry

    def group1(j, carry):
        new = block_stats(*load_block(q0, k0, v0, 0, j, seq))
        rows = pl.ds(pl.multiple_of(j * Q_BLOCK, Q_BLOCK), Q_BLOCK)
        _, l, acc = merge([s1_ref[t, rows, :] for t in range(3)], new)
        o_ref[0, rows, lanes] = (acc / l).astype(o_ref.dtype)
        return carry

    n_blocks = seq // Q_BLOCK
    lax.fori_loop(0, n_blocks, group16, 0, unroll=ATTN_UNROLL)
    lax.fori_loop(0, n_blocks, group4, 0, unroll=ATTN_UNROLL)
    lax.fori_loop(0, n_blocks, group1, 0, unroll=ATTN_UNROLL)


def _attention(qkv_groups, seq, cast_weights):
    b = qkv_groups[0].shape[0]
    d = D_MODEL
    width = ATTN_PAIRS * LANES
    n_steps = d // width
    in_specs, args = [], []
    for qkv, (_, dil) in zip(qkv_groups, DILATED_GROUPS):
        for part in range(3):
            in_specs.append(pl.BlockSpec((1, dil, seq // dil, width),
                                         lambda i, p, part=part: (i, 0, 0, part * n_steps + p)))
            args.append(qkv)
    wide = Q_BLOCK + 2 * HALF_WINDOW
    assert seq // DILATED_GROUPS[2][1] == Q_BLOCK and seq // DILATED_GROUPS[1][1] >= wide
    in_specs += [pl.BlockSpec((3, 2 * Q_BLOCK, wide), lambda i, p: (0, 0, 0)),
                 pl.BlockSpec((2, 2 * Q_BLOCK, wide), lambda i, p: (0, 0, 0))]
    args += [jnp.asarray(_band_bias(wide)), jnp.asarray(_pair_bias())]
    c_in, c_out, c_shapes, c_args, c_pers = _cast_specs(cast_weights, n_steps, b * n_steps)
    return pl.pallas_call(
        functools.partial(_attention_kernel, seq=seq, cast_pers=c_pers),
        grid=(b, n_steps),
        in_specs=in_specs + c_in,
        out_specs=[pl.BlockSpec((1, seq, width), lambda i, p: (i, 0, p))] + c_out,
        out_shape=[jax.ShapeDtypeStruct((b, seq, d), _BF16)] + c_shapes,
        scratch_shapes=[pltpu.VMEM((ATTN_PAIRS, 3, seq, LANES), _F32),
                        pltpu.VMEM((ATTN_PAIRS, 3, seq, LANES), _F32)],
        compiler_params=_compiler_params(2),
        name="dilated_attention",
    )(*args, *c_args)


FFN_CHUNK = MXU_WIDTH


TAIL_ROWS = 1024
TAIL_SUB_ROWS = 256


def _ffn_norm(x1, mod_ref, g_ref, out_ref, h_ref, rows):
    d = D_MODEL
    out_ref[0, rows, :] = x1
    h_ref[rows, :] = _modulated_norm(x1, g_ref[...], mod_ref[:, 3 * d:4 * d], mod_ref[:, 4 * d:5 * d]).astype(_BF16)


def _ffn_matmuls(mod_ref, w_in_ref, w_out_ref, out_ref, h_ref, act_ref, rows, finish=None):
    d, f = D_MODEL, FFN_HIDDEN
    h = h_ref[rows, :]
    for c in range(f // FFN_CHUNK):
        lo = c * FFN_CHUNK
        gate = jnp.dot(h, w_in_ref[:, lo:lo + FFN_CHUNK], preferred_element_type=_F32)
        up = jnp.dot(h, w_in_ref[:, f + lo:f + lo + FFN_CHUNK], preferred_element_type=_F32)
        act_ref[rows, lo:lo + FFN_CHUNK] = (gate * jax.nn.sigmoid(gate) * up).astype(_BF16)
    y = jnp.dot(act_ref[rows, :], w_out_ref[...], preferred_element_type=_F32)
    res = out_ref[0, rows, :] + mod_ref[:, 5 * d:6 * d] * y
    out_ref[0, rows, :] = res if finish is None else finish(res)


def _tail_rows(sub):
    return slice(sub * TAIL_SUB_ROWS, (sub + 1) * TAIL_SUB_ROWS)


def _post_attention_kernel(x_ref, o_ref, mod_ref, w_o_ref, g_ref, w_in_ref, w_out_ref, out_ref, act_ref, h_ref):
    d = D_MODEL

    def stage_a(sub):
        rows = _tail_rows(sub)
        y = jnp.dot(o_ref[0, rows, :], w_o_ref[...], preferred_element_type=_F32)
        _ffn_norm(x_ref[0, rows, :] + mod_ref[:, 2 * d:3 * d] * y, mod_ref, g_ref, out_ref, h_ref, rows)

    def stage_b(sub):
        _ffn_matmuls(mod_ref, w_in_ref, w_out_ref, out_ref, h_ref, act_ref, _tail_rows(sub))

    _staggered(TAIL_ROWS // TAIL_SUB_ROWS, stage_a, stage_b)


def _post_attention(x, o, mod, w_o, g, w_in, w_out):
    b, s, d = x.shape
    f = FFN_HIDDEN
    tm = TAIL_ROWS
    return pl.pallas_call(
        _post_attention_kernel,
        grid=(b, s // tm),
        in_specs=[
            pl.BlockSpec((1, tm, d), lambda i, j: (i, j, 0)),
            pl.BlockSpec((1, tm, d), lambda i, j: (i, j, 0)),
            pl.BlockSpec((None, 1, 6 * d), lambda i, j: (i, 0, 0)),
            _resident((d, d), lambda i, j: (0, 0)),
            pl.BlockSpec((1, d), lambda i, j: (0, 0)),
            _resident((d, 2 * f), lambda i, j: (0, 0)),
            _resident((f, d), lambda i, j: (0, 0)),
        ],
        out_specs=pl.BlockSpec((1, tm, d), lambda i, j: (i, j, 0)),
        out_shape=jax.ShapeDtypeStruct((b, s, d), _F32),
        scratch_shapes=[pltpu.VMEM((tm, f), _BF16), pltpu.VMEM((tm, d), _BF16)],
        compiler_params=_compiler_params(2),
        name="post_attention",
    )(x, o, mod, w_o, g, w_in, w_out)


CONV_IN_ROWS = 1024
CONV_IN_SUB_ROWS = 256


def _conv_in_kernel(x_ref, mod_ref, g_ref, w_ref, gate_ref, cu_ref, h_ref):
    d = D_MODEL

    def group_rows(sub):
        return slice(sub * CONV_IN_SUB_ROWS, (sub + 1) * CONV_IN_SUB_ROWS)

    def stage_a(sub):
        rows = group_rows(sub)
        h_ref[rows, :] = _modulated_norm(
            x_ref[0, rows, :], g_ref[...], mod_ref[:, 0:d], mod_ref[:, d:2 * d]).astype(_BF16)

    def stage_b(sub):
        rows = group_rows(sub)
        h = h_ref[rows, :]
        for j in range(d // MXU_WIDTH):
            lo = j * MXU_WIDTH
            gate_ref[0, rows, lo:lo + MXU_WIDTH] = jnp.dot(
                h, w_ref[:, lo:lo + MXU_WIDTH], preferred_element_type=_F32).astype(_BF16)
            c_gate = jnp.dot(h, w_ref[:, d + lo:d + lo + MXU_WIDTH], preferred_element_type=_F32)
            u = jnp.dot(h, w_ref[:, 2 * d + lo:2 * d + lo + MXU_WIDTH], preferred_element_type=_F32)
            cu_ref[0, rows, lo:lo + MXU_WIDTH] = (c_gate * u).astype(_BF16)

    _staggered(CONV_IN_ROWS // CONV_IN_SUB_ROWS, stage_a, stage_b)


def _conv_in(x, mod, g, w):
    b, s, d = x.shape
    tm = CONV_IN_ROWS
    tile = pl.BlockSpec((1, tm, d), lambda i, j: (i, j, 0))
    return pl.pallas_call(
        _conv_in_kernel,
        grid=(b, s // tm),
        in_specs=[
            tile,
            pl.BlockSpec((None, 1, 6 * d), lambda i, j: (i, 0, 0)),
            pl.BlockSpec((1, d), lambda i, j: (0, 0)),
            _resident((d, 3 * d), lambda i, j: (0, 0)),
        ],
        out_specs=[tile, tile],
        out_shape=[jax.ShapeDtypeStruct((b, s, d), _BF16)] * 2,
        scratch_shapes=[pltpu.VMEM((tm, d), _BF16)],
        compiler_params=_compiler_params(2),
        name="conv_in",
    )(x, mod, g, w)


HALO_ROWS = 16


def _conv_out_kernel(x_ref, gate_ref, cu_ref, prev_ref, next_ref, mod_ref, cw_ref, w_o_ref, g_ref,
                     w_in_ref, w_out_ref, gf_ref, out_ref, act_ref, h_ref, *, tm):
    d = D_MODEL
    j = pl.program_id(1)
    n_sub = tm // TAIL_SUB_ROWS
    row = lax.broadcasted_iota(jnp.int32, (TAIL_SUB_ROWS, d), 0)

    def stage_a(sub):
        lo, hi = sub * TAIL_SUB_ROWS, (sub + 1) * TAIL_SUB_ROWS
        rows = slice(lo, hi)
        cu = cu_ref[0, rows, :].astype(_F32)
        if sub == 0:
            prev_row = jnp.where(j > 0, prev_ref[0, HALO_ROWS - 1:HALO_ROWS, :].astype(_F32), 0.0)
        else:
            prev_row = cu_ref[0, lo - 1:lo, :].astype(_F32)
        if sub == n_sub - 1:
            next_row = jnp.where(j < pl.num_programs(1) - 1, next_ref[0, 0:1, :].astype(_F32), 0.0)
        else:
            next_row = cu_ref[0, hi:hi + 1, :].astype(_F32)
        before = jnp.where(row == 0, prev_row, pltpu.roll(cu, 1, 0))
        after = jnp.where(row == TAIL_SUB_ROWS - 1, next_row, pltpu.roll(cu, TAIL_SUB_ROWS - 1, 0))
        z = before * cw_ref[0:1, :] + cu * cw_ref[1:2, :] + after * cw_ref[2:3, :]
        gated = (gate_ref[0, rows, :].astype(_F32) * z).astype(_BF16)
        y = jnp.dot(gated, w_o_ref[...], preferred_element_type=_F32)
        _ffn_norm(x_ref[0, rows, :] + mod_ref[:, 2 * d:3 * d] * y, mod_ref, g_ref, out_ref, h_ref, rows)

    def stage_b(sub):
        _ffn_matmuls(mod_ref, w_in_ref, w_out_ref, out_ref, h_ref, act_ref, _tail_rows(sub),
                     finish=lambda x2: _rms_norm(x2, gf_ref[...]))

    _staggered(n_sub, stage_a, stage_b)


def _conv_out(x, gate, cu, mod, conv_w, w_o, g, w_in, w_out, final_g):
    b, s, d = x.shape
    f = FFN_HIDDEN
    tm = TAIL_ROWS
    per_tile = tm // HALO_ROWS
    n_halo = s // HALO_ROWS
    tile = pl.BlockSpec((1, tm, d), lambda i, j: (i, j, 0))
    return pl.pallas_call(
        functools.partial(_conv_out_kernel, tm=tm),
        grid=(b, s // tm),
        in_specs=[
            tile, tile, tile,
            pl.BlockSpec((1, HALO_ROWS, d), lambda i, j: (i, jnp.maximum(j * per_tile - 1, 0), 0)),
            pl.BlockSpec((1, HALO_ROWS, d), lambda i, j: (i, jnp.minimum((j + 1) * per_tile, n_halo - 1), 0)),
            pl.BlockSpec((None, 1, 6 * d), lambda i, j: (i, 0, 0)),
            pl.BlockSpec((CONV_WIDTH, d), lambda i, j: (0, 0)),
            _resident((d, d), lambda i, j: (0, 0)),
            pl.BlockSpec((1, d), lambda i, j: (0, 0)),
            _resident((d, 2 * f), lambda i, j: (0, 0)),
            _resident((f, d), lambda i, j: (0, 0)),
            pl.BlockSpec((1, d), lambda i, j: (0, 0)),
        ],
        out_specs=tile,
        out_shape=jax.ShapeDtypeStruct((b, s, d), _F32),
        scratch_shapes=[pltpu.VMEM((tm, f), _BF16), pltpu.VMEM((tm, d), _BF16)],
        compiler_params=_compiler_params(2),
        name="conv_out",
    )(x, gate, cu, cu, cu, mod, conv_w, w_o, g, w_in, w_out, final_g)


def kernel(x, c, attn_w_qkv, attn_w_o, conv_w_in, conv_w, conv_w_out, ada_w, ada_b,
           norm_mix_g, norm_ffn_g, ffn_w_in, ffn_w_out, final_g):
    b, s, d = x.shape
    assert d == D_MODEL and attn_w_qkv.shape[0] == 1 and conv_w_in.shape[0] == 1 and ada_w.shape[0] == 2
    mod, w_qkv = _modulation(c, ada_w, ada_b, attn_w_qkv)
    mod = mod.reshape(2, b, 1, 6 * d)

    qkv = _qkv(x, mod[0], norm_mix_g[0:1], w_qkv)
    o, w_o, w_in0, w_out0, w_ci, w_co, w_in1, w_out1 = _attention(
        qkv, s, [(attn_w_o, 0), (ffn_w_in, 0), (ffn_w_out, 0),
                 (conv_w_in, 0), (conv_w_out, 0), (ffn_w_in, 1), (ffn_w_out, 1)])
    x = _post_attention(x, o, mod[0], w_o, norm_ffn_g[0:1], w_in0, w_out0)

    gate, cu = _conv_in(x, mod[1], norm_mix_g[1:2], w_ci)
    return _conv_out(x, gate, cu, mod[1], conv_w[0], w_co, norm_ffn_g[1:2],
                     w_in1, w_out1, final_g.reshape(1, d))
```

```python
import functools

import numpy as np
import jax
import jax.numpy as jnp
from jax import lax
from jax.experimental import pallas as pl
from jax.experimental.pallas import tpu as pltpu

D_MODEL = 1024
HEAD_DIM = 64
N_HEADS = D_MODEL // HEAD_DIM
DILATED_GROUPS = ((128, 1), (512, 4), (2048, 16))
ROPE_THETA = 500000.0
ROPE_DIM = HEAD_DIM // 4
CONV_WIDTH = 3
FFN_HIDDEN = -(-8 * D_MODEL // (3 * 256)) * 256
NORM_EPS = 1e-6
NEG_INF = -1e30

LANES = 128
MXU_WIDTH = 256
HALF_WINDOW = 64
Q_BLOCK = 128
SCORE_SCALE = HEAD_DIM ** -0.5 * float(np.log2(np.e))
VMEM_LIMIT_BYTES = 56 * 1024 * 1024

assert all(w // (2 * d) == HALF_WINDOW for w, d in DILATED_GROUPS)

_F32 = jnp.float32
_BF16 = jnp.bfloat16


def _compiler_params(n_grid_dims):
    return pltpu.CompilerParams(
        dimension_semantics=("arbitrary",) * n_grid_dims,
        vmem_limit_bytes=VMEM_LIMIT_BYTES,
    )


def _resident(block_shape, index_map):
    return pl.BlockSpec(block_shape, index_map, pipeline_mode=pl.Buffered(1))


def _rms_norm(x, g):
    r = lax.rsqrt(jnp.mean(x * x, axis=-1, keepdims=True) + NORM_EPS)
    return (x * r) * g


def _modulated_norm(x, g, shift, scale):
    return _rms_norm(x, g) * (1.0 + scale) + shift


MODULATION_COLS = 1536


def _modulation_kernel(c_ref, w_ref, b_ref, wq_ref, o_ref, wq_bf_ref):
    c = c_ref[...]
    cond = (c * jax.nn.sigmoid(c)).astype(_BF16)
    o_ref[...] = jnp.dot(cond, w_ref[...].astype(_BF16), preferred_element_type=_F32) + b_ref[...]
    wq_bf_ref[...] = wq_ref[...].astype(_BF16)


def _modulation(c, ada_w, ada_b, w_qkv):
    depth, d, n = ada_w.shape
    b = c.shape[0]
    tn = MODULATION_COLS
    n_j = n // tn
    rows, cols = w_qkv.shape[1:]
    wq_rows = rows // (depth * n_j)
    assert wq_rows % CAST_ROWS == 0
    return pl.pallas_call(
        _modulation_kernel,
        grid=(depth, n_j),
        in_specs=[
            pl.BlockSpec((b, d), lambda i, j: (0, 0)),
            pl.BlockSpec((None, d, tn), lambda i, j: (i, 0, j)),
            pl.BlockSpec((None, 1, tn), lambda i, j: (i, 0, j)),
            pl.BlockSpec((None, wq_rows, cols), lambda i, j: (0, i * n_j + j, 0)),
        ],
        out_specs=[pl.BlockSpec((None, b, tn), lambda i, j: (i, 0, j)),
                   pl.BlockSpec((wq_rows, cols), lambda i, j: (i * n_j + j, 0))],
        out_shape=[jax.ShapeDtypeStruct((depth, b, n), _F32), jax.ShapeDtypeStruct((rows, cols), _BF16)],
        compiler_params=_compiler_params(2),
        name="modulation",
    )(c, ada_w, ada_b.reshape(depth, 1, n), w_qkv)


CAST_ROWS = 16


def _cast_specs(weights, n_inner, total_steps):
    in_specs, out_specs, out_shapes, args, pers = [], [], [], [], []
    for w, layer in weights:
        rows, cols = w.shape[1:]
        n_blocks = max(nb for nb in range(1, total_steps + 1)
                       if total_steps % nb == 0 and rows % (nb * CAST_ROWS) == 0)
        block_rows = rows // n_blocks
        block = lambda i, j, per=total_steps // n_blocks: (i * n_inner + j) // per
        in_specs.append(pl.BlockSpec((None, block_rows, cols),
                                     lambda i, j, layer=layer, block=block: (layer, block(i, j), 0)))
        out_specs.append(pl.BlockSpec((block_rows, cols), lambda i, j, block=block: (block(i, j), 0)))
        out_shapes.append(jax.ShapeDtypeStruct((rows, cols), _BF16))
        args.append(w)
        pers.append(total_steps // n_blocks)
    return in_specs, out_specs, out_shapes, args, tuple(pers)


CAST_GUARD_STEPS = 4


def _cast_blocks(in_refs, out_refs, pers, step, guarded):
    for w_ref, o_ref, per in zip(in_refs, out_refs, pers):
        if guarded and per >= CAST_GUARD_STEPS:
            @pl.when(step % per == 0)
            def _(w_ref=w_ref, o_ref=o_ref):
                o_ref[...] = w_ref[...].astype(_BF16)
        elif not guarded and per < CAST_GUARD_STEPS:
            o_ref[...] = w_ref[...].astype(_BF16)


def _rope_tables(seq, dilation):
    half = ROPE_DIM // 2
    length = seq // dilation
    inv = ROPE_THETA ** (-np.arange(half, dtype=np.float64) * (2.0 / ROPE_DIM))
    pos = (np.arange(length)[None, :] * dilation + np.arange(dilation)[:, None]).astype(np.float64)
    ang = pos[:, :, None] * inv[None, None, :]
    cos, sin = np.cos(ang), np.sin(ang)
    c = np.ones((dilation, length, HEAD_DIM))
    sa = np.zeros((dilation, length, HEAD_DIM))
    sb = np.zeros((dilation, length, HEAD_DIM))
    c[:, :, :half] = cos
    c[:, :, half:ROPE_DIM] = cos
    sa[:, :, :half] = -sin
    sb[:, :, half:ROPE_DIM] = sin
    return np.stack([np.tile(t, (1, 1, LANES // HEAD_DIM)) for t in (c, sa, sb)]).astype(np.float32)


QKV_ROWS = 512
QKV_SUB_ROWS = 256


def _staggered(n_groups, stage_a, stage_b):
    stage_a(0)
    for k in range(n_groups):
        if k + 1 < n_groups:
            stage_a(k + 1)
        stage_b(k)


def _qkv_kernel(x_ref, mod_ref, g_ref, w_ref, tab1_ref, tab4_ref, tab16_ref, o1_ref, o4_ref, o16_ref,
                lhs_ref, hs1_ref, hs4_ref):
    d, tm = D_MODEL, QKV_SUB_ROWS
    d4 = DILATED_GROUPS[1][1]
    n_slabs = d // LANES

    def stage_a(sub):
        x = x_ref[0, sub * tm:(sub + 1) * tm, :]
        h = _modulated_norm(x, g_ref[...], mod_ref[:, 0:d], mod_ref[:, d:2 * d])
        for c in range(n_slabs):
            hs1_ref[c] = h[:, c * LANES:(c + 1) * LANES]
        rows4 = tm // d4
        for c in range(n_slabs):
            hs4_ref[c] = jnp.concatenate([hs1_ref[c, pl.ds(r, rows4, stride=d4), :] for r in range(d4)], axis=0)
        h4 = jnp.concatenate([hs4_ref[c] for c in range(n_slabs)], axis=1)
        rows16 = rows4 // d4
        h16 = jnp.concatenate(
            [jnp.concatenate([hs4_ref[c, pl.ds((r16 % d4) * rows4 + r16 // d4, rows16, stride=d4), :]
                              for c in range(n_slabs)], axis=1)
             for r16 in range(d4 * d4)], axis=0)
        for g, hg in enumerate((h, h4, h16)):
            lhs_ref[sub, g] = hg.astype(_BF16)

    def stage_b(sub):
        for g, (tab_ref, o_ref) in enumerate(((tab1_ref, o1_ref), (tab4_ref, o4_ref), (tab16_ref, o16_ref))):
            dilation = DILATED_GROUPS[g][1]
            rows = tm // dilation
            out_rows = slice(sub * rows, (sub + 1) * rows)
            hb = lhs_ref[sub, g]
            for part in range(3):
                for j in range(d // MXU_WIDTH):
                    col = part * d + j * MXU_WIDTH
                    res = jnp.dot(hb, w_ref[:, g * 3 * d + col:g * 3 * d + col + MXU_WIDTH],
                                  preferred_element_type=_F32)
                    if part < 2:
                        halves = []
                        for s in range(MXU_WIDTH // LANES):
                            xs = res[:, s * LANES:(s + 1) * LANES]
                            tabs = [jnp.concatenate([tab_ref[t, r, out_rows, :] for r in range(dilation)], axis=0)
                                    for t in range(3)]
                            halves.append(xs * tabs[0]
                                          + pltpu.roll(xs, LANES - ROPE_DIM // 2, 1) * tabs[1]
                                          + pltpu.roll(xs, ROPE_DIM // 2, 1) * tabs[2])
                        res = jnp.concatenate(halves, axis=1)
                        if part == 0:
                            res = res * SCORE_SCALE
                    res = res.astype(_BF16)
                    for r in range(dilation):
                        o_ref[0, r, out_rows, col:col + MXU_WIDTH] = res[r * rows:(r + 1) * rows]

    _staggered(QKV_ROWS // QKV_SUB_ROWS, stage_a, stage_b)


def _qkv(x, mod, g, w):
    b, s, d = x.shape
    tm = QKV_ROWS
    dils = [dil for _, dil in DILATED_GROUPS]
    assert dils == [1, 4, 16]
    tabs = [jnp.asarray(_rope_tables(s, dil)) for dil in dils]
    return pl.pallas_call(
        _qkv_kernel,
        grid=(b, s // tm),
        in_specs=[
            pl.BlockSpec((1, tm, d), lambda i, j: (i, j, 0)),
            pl.BlockSpec((None, 1, 6 * d), lambda i, j: (i, 0, 0)),
            pl.BlockSpec((1, d), lambda i, j: (0, 0)),
            _resident(w.shape, lambda i, j: (0, 0)),
        ] + [pl.BlockSpec((3, dil, tm // dil, LANES), lambda i, j: (0, 0, j, 0)) for dil in dils],
        out_specs=[pl.BlockSpec((1, dil, tm // dil, 3 * d), lambda i, j: (i, 0, j, 0)) for dil in dils],
        out_shape=[jax.ShapeDtypeStruct((b, dil, s // dil, 3 * d), _BF16) for dil in dils],
        scratch_shapes=[pltpu.VMEM((QKV_ROWS // QKV_SUB_ROWS, len(dils), QKV_SUB_ROWS, d), _BF16)]
        + [pltpu.VMEM((d // LANES, QKV_SUB_ROWS, LANES), _F32)] * 2,
        compiler_params=_compiler_params(2),
        name="qkv",
    )(x, mod, g, w, *tabs)


def _band_bias(n_keys):
    i = np.arange(Q_BLOCK)[:, None]
    j = np.arange(n_keys)[None, :]
    one = np.stack([np.where(np.abs(off + i - j) <= HALF_WINDOW, 0.0, NEG_INF)
                    for off in (0, HALF_WINDOW, 2 * HALF_WINDOW)])
    return np.concatenate([one, one], axis=1).astype(np.float32)


def _pair_bias():
    band = _band_bias(Q_BLOCK)[0]
    masked = np.full_like(band, NEG_INF)
    return np.stack([np.concatenate([band, masked], axis=1), np.concatenate([masked, band], axis=1)])


ATTN_UNROLL = 16
ATTN_PAIRS = 2


def _attention_kernel(q0, k0, v0, q1, k1, v1, q2, k2, v2, bias_w_ref, bias_p_ref, *refs, seq, cast_pers):
    n_cast = len(cast_pers)
    cast_in, o_ref = refs[:n_cast], refs[n_cast]
    cast_out, (s4_ref, s1_ref, s0_ref) = refs[n_cast + 1:2 * n_cast + 1], refs[2 * n_cast + 1:]
    step = pl.program_id(0) * pl.num_programs(1) + pl.program_id(1)
    _cast_blocks(cast_in, cast_out, cast_pers, step, True)
    lane = lax.broadcasted_iota(jnp.int32, (Q_BLOCK, LANES), 1)
    head_a = lane < HEAD_DIM
    for pair in range(ATTN_PAIRS):
        if pair == 1:
            _cast_blocks(cast_in, cast_out, cast_pers, step, False)
        _attention_pair(q0, k0, v0, q1, k1, v1, q2, k2, v2, bias_w_ref, bias_p_ref, o_ref,
                        s4_ref.at[pair], s1_ref.at[pair], s0_ref.at[pair], pl.ds(pair * LANES, LANES), head_a, seq)


def _attention_pair(q0, k0, v0, q1, k1, v1, q2, k2, v2, bias_w_ref, bias_p_ref, o_ref,
                    s4_ref, s1_ref, s0_ref, lanes, head_a, seq):
    d4 = DILATED_GROUPS[1][1]
    len4 = seq // d4

    def block_stats(q, k, v, bias):
        n_keys = k.shape[0]
        zero = jnp.zeros_like(q)
        q2 = jnp.concatenate([jnp.where(head_a, q, zero), jnp.where(head_a, zero, q)], axis=0)
        s = lax.dot_general(q2, k, (((1,), (1,)), ((), ())), preferred_element_type=_F32) + bias
        m = jnp.max(s, axis=-1, keepdims=True)
        p = jnp.exp2(s - m).astype(_BF16)
        v_ext = jnp.concatenate([v, jnp.ones((n_keys, LANES), _BF16)], axis=1)
        pv = jnp.dot(p, v_ext, preferred_element_type=_F32)
        m_pair = jnp.where(head_a, jnp.broadcast_to(m[:Q_BLOCK], (Q_BLOCK, LANES)),
                           jnp.broadcast_to(m[Q_BLOCK:], (Q_BLOCK, LANES)))
        l_pair = jnp.where(head_a, pv[:Q_BLOCK, LANES:], pv[Q_BLOCK:, LANES:])
        acc_pair = jnp.where(head_a, pv[:Q_BLOCK, :LANES], pv[Q_BLOCK:, :LANES])
        return m_pair, l_pair, acc_pair

    def merge(old, new):
        m = jnp.maximum(old[0], new[0])
        a_old = jnp.exp2(old[0] - m)
        a_new = jnp.exp2(new[0] - m)
        return m, a_old * old[1] + a_new * new[1], a_old * old[2] + a_new * new[2]

    def load_block(q_ref, k_ref, v_ref, r, j, length):
        n_keys = min(length, Q_BLOCK + 2 * HALF_WINDOW)
        m0 = pl.multiple_of(j * Q_BLOCK, Q_BLOCK)
        k_start = pl.multiple_of(jnp.clip(m0 - HALF_WINDOW, 0, length - n_keys), HALF_WINDOW)
        bias = bias_w_ref[(m0 - k_start) // HALF_WINDOW]
        return (q_ref[0, r, pl.ds(m0, Q_BLOCK), lanes], k_ref[0, r, pl.ds(k_start, n_keys), lanes],
                v_ref[0, r, pl.ds(k_start, n_keys), lanes], bias)

    def load_pair_block(q_ref, k_ref, v_ref, r):
        first = pl.multiple_of((r // 2) * 2, 2)
        keys = lambda ref: ref[0, pl.ds(first, 2), :, lanes].reshape(2 * Q_BLOCK, LANES)
        return q_ref[0, r, :, lanes], keys(k_ref), keys(v_ref), bias_p_ref[r % 2]

    def group16(idx, carry):
        a, c = idx // d4, idx % d4
        state = block_stats(*load_pair_block(q2, k2, v2, idx))
        rows = pl.ds(c * len4 + a, Q_BLOCK, stride=d4)
        for t in range(3):
            s4_ref[t, rows, :] = state[t]
        return carry

    def group4(idx, carry):
        c, j = idx // (len4 // Q_BLOCK), idx % (len4 // Q_BLOCK)
        new = block_stats(*load_block(q1, k1, v1, c, j, len4))
        src = pl.ds(pl.multiple_of(c * len4 + j * Q_BLOCK, Q_BLOCK), Q_BLOCK)
        state = merge([s4_ref[t, src, :] for t in range(3)], new)
        rows = pl.ds(j * Q_BLOCK * d4 + c, Q_BLOCK, stride=d4)
        for t in range(3):
            s1_ref[t, rows, :] = state[t]
        return carry

    def group1(j, carry):
        new = block_stats(*load_block(q0, k0, v0, 0, j, seq))
        rows = pl.ds(pl.multiple_of(j * Q_BLOCK, Q_BLOCK), Q_BLOCK)
        for t in range(3):
            s0_ref[t, rows, :] = new[t]
        return carry

    def finish(j, carry):
        rows = pl.ds(pl.multiple_of(j * Q_BLOCK, Q_BLOCK), Q_BLOCK)
        _, l, acc = merge([s1_ref[t, rows, :] for t in range(3)], [s0_ref[t, rows, :] for t in range(3)])
        o_ref[0, rows, lanes] = (acc / l).astype(o_ref.dtype)
        return carry

    n_blocks = seq // Q_BLOCK
    lax.fori_loop(0, n_blocks, group1, 0, unroll=ATTN_UNROLL)
    lax.fori_loop(0, n_blocks, group16, 0, unroll=ATTN_UNROLL)
    lax.fori_loop(0, n_blocks, group4, 0, unroll=ATTN_UNROLL)
    lax.fori_loop(0, n_blocks, finish, 0, unroll=ATTN_UNROLL)


def _attention(qkv_groups, seq, cast_weights):
    b = qkv_groups[0].shape[0]
    d = D_MODEL
    width = ATTN_PAIRS * LANES
    n_steps = d // width
    in_specs, args = [], []
    for qkv, (_, dil) in zip(qkv_groups, DILATED_GROUPS):
        for part in range(3):
            in_specs.append(pl.BlockSpec((1, dil, seq // dil, width),
                                         lambda i, p, part=part: (i, 0, 0, part * n_steps + p)))
            args.append(qkv)
    wide = Q_BLOCK + 2 * HALF_WINDOW
    assert seq // DILATED_GROUPS[2][1] == Q_BLOCK and seq // DILATED_GROUPS[1][1] >= wide
    in_specs += [pl.BlockSpec((3, 2 * Q_BLOCK, wide), lambda i, p: (0, 0, 0)),
                 pl.BlockSpec((2, 2 * Q_BLOCK, wide), lambda i, p: (0, 0, 0))]
    args += [jnp.asarray(_band_bias(wide)), jnp.asarray(_pair_bias())]
    c_in, c_out, c_shapes, c_args, c_pers = _cast_specs(cast_weights, n_steps, b * n_steps)
    return pl.pallas_call(
        functools.partial(_attention_kernel, seq=seq, cast_pers=c_pers),
        grid=(b, n_steps),
        in_specs=in_specs + c_in,
        out_specs=[pl.BlockSpec((1, seq, width), lambda i, p: (i, 0, p))] + c_out,
        out_shape=[jax.ShapeDtypeStruct((b, seq, d), _BF16)] + c_shapes,
        scratch_shapes=[pltpu.VMEM((ATTN_PAIRS, 3, seq, LANES), _F32)] * 3,
        compiler_params=_compiler_params(2),
        name="dilated_attention",
    )(*args, *c_args)


FFN_CHUNK = MXU_WIDTH


TAIL_ROWS = 1024
TAIL_SUB_ROWS = 256


def _ffn_norm(x1, mod_ref, g_ref, out_ref, h_ref, rows):
    d = D_MODEL
    out_ref[0, rows, :] = x1
    h_ref[rows, :] = _modulated_norm(x1, g_ref[...], mod_ref[:, 3 * d:4 * d], mod_ref[:, 4 * d:5 * d]).astype(_BF16)


def _ffn_matmuls(mod_ref, w_in_ref, w_out_ref, out_ref, h_ref, act_ref, rows, finish=None):
    d, f = D_MODEL, FFN_HIDDEN
    h = h_ref[rows, :]
    for c in range(f // FFN_CHUNK):
        lo = c * FFN_CHUNK
        gate = jnp.dot(h, w_in_ref[:, lo:lo + FFN_CHUNK], preferred_element_type=_F32)
        up = jnp.dot(h, w_in_ref[:, f + lo:f + lo + FFN_CHUNK], preferred_element_type=_F32)
        act_ref[rows, lo:lo + FFN_CHUNK] = (gate * jax.nn.sigmoid(gate) * up).astype(_BF16)
    y = jnp.dot(act_ref[rows, :], w_out_ref[...], preferred_element_type=_F32)
    res = out_ref[0, rows, :] + mod_ref[:, 5 * d:6 * d] * y
    out_ref[0, rows, :] = res if finish is None else finish(res)


def _tail_rows(sub):
    return slice(sub * TAIL_SUB_ROWS, (sub + 1) * TAIL_SUB_ROWS)


def _post_attention_kernel(x_ref, o_ref, mod_ref, w_o_ref, g_ref, w_in_ref, w_out_ref, out_ref, act_ref, h_ref):
    d = D_MODEL

    def stage_a(sub):
        rows = _tail_rows(sub)
        y = jnp.dot(o_ref[0, rows, :], w_o_ref[...], preferred_element_type=_F32)
        _ffn_norm(x_ref[0, rows, :] + mod_ref[:, 2 * d:3 * d] * y, mod_ref, g_ref, out_ref, h_ref, rows)

    def stage_b(sub):
        _ffn_matmuls(mod_ref, w_in_ref, w_out_ref, out_ref, h_ref, act_ref, _tail_rows(sub))

    _staggered(TAIL_ROWS // TAIL_SUB_ROWS, stage_a, stage_b)


def _post_attention(x, o, mod, w_o, g, w_in, w_out):
    b, s, d = x.shape
    f = FFN_HIDDEN
    tm = TAIL_ROWS
    return pl.pallas_call(
        _post_attention_kernel,
        grid=(b, s // tm),
        in_specs=[
            pl.BlockSpec((1, tm, d), lambda i, j: (i, j, 0)),
            pl.BlockSpec((1, tm, d), lambda i, j: (i, j, 0)),
            pl.BlockSpec((None, 1, 6 * d), lambda i, j: (i, 0, 0)),
            _resident((d, d), lambda i, j: (0, 0)),
            pl.BlockSpec((1, d), lambda i, j: (0, 0)),
            _resident((d, 2 * f), lambda i, j: (0, 0)),
            _resident((f, d), lambda i, j: (0, 0)),
        ],
        out_specs=pl.BlockSpec((1, tm, d), lambda i, j: (i, j, 0)),
        out_shape=jax.ShapeDtypeStruct((b, s, d), _F32),
        scratch_shapes=[pltpu.VMEM((tm, f), _BF16), pltpu.VMEM((tm, d), _BF16)],
        compiler_params=_compiler_params(2),
        name="post_attention",
    )(x, o, mod, w_o, g, w_in, w_out)


CONV_IN_ROWS = 1024
CONV_IN_SUB_ROWS = 256


def _conv_in_kernel(x_ref, mod_ref, g_ref, w_ref, gate_ref, cu_ref, h_ref):
    d = D_MODEL

    def group_rows(sub):
        return slice(sub * CONV_IN_SUB_ROWS, (sub + 1) * CONV_IN_SUB_ROWS)

    def stage_a(sub):
        rows = group_rows(sub)
        h_ref[rows, :] = _modulated_norm(
            x_ref[0, rows, :], g_ref[...], mod_ref[:, 0:d], mod_ref[:, d:2 * d]).astype(_BF16)

    def stage_b(sub):
        rows = group_rows(sub)
        h = h_ref[rows, :]
        for j in range(d // MXU_WIDTH):
            lo = j * MXU_WIDTH
            gate_ref[0, rows, lo:lo + MXU_WIDTH] = jnp.dot(
                h, w_ref[:, lo:lo + MXU_WIDTH], preferred_element_type=_F32).astype(_BF16)
            c_gate = jnp.dot(h, w_ref[:, d + lo:d + lo + MXU_WIDTH], preferred_element_type=_F32)
            u = jnp.dot(h, w_ref[:, 2 * d + lo:2 * d + lo + MXU_WIDTH], preferred_element_type=_F32)
            cu_ref[0, rows, lo:lo + MXU_WIDTH] = (c_gate * u).astype(_BF16)

    _staggered(CONV_IN_ROWS // CONV_IN_SUB_ROWS, stage_a, stage_b)


def _conv_in(x, mod, g, w):
    b, s, d = x.shape
    tm = CONV_IN_ROWS
    tile = pl.BlockSpec((1, tm, d), lambda i, j: (i, j, 0))
    return pl.pallas_call(
        _conv_in_kernel,
        grid=(b, s // tm),
        in_specs=[
            tile,
            pl.BlockSpec((None, 1, 6 * d), lambda i, j: (i, 0, 0)),
            pl.BlockSpec((1, d), lambda i, j: (0, 0)),
            _resident((d, 3 * d), lambda i, j: (0, 0)),
        ],
        out_specs=[tile, tile],
        out_shape=[jax.ShapeDtypeStruct((b, s, d), _BF16)] * 2,
        scratch_shapes=[pltpu.VMEM((tm, d), _BF16)],
        compiler_params=_compiler_params(2),
        name="conv_in",
    )(x, mod, g, w)


HALO_ROWS = 16


def _conv_out_kernel(x_ref, gate_ref, cu_ref, prev_ref, next_ref, mod_ref, cw_ref, w_o_ref, g_ref,
                     w_in_ref, w_out_ref, gf_ref, out_ref, act_ref, h_ref, *, tm):
    d = D_MODEL
    j = pl.program_id(1)
    n_sub = tm // TAIL_SUB_ROWS
    row = lax.broadcasted_iota(jnp.int32, (TAIL_SUB_ROWS, d), 0)

    def stage_a(sub):
        lo, hi = sub * TAIL_SUB_ROWS, (sub + 1) * TAIL_SUB_ROWS
        rows = slice(lo, hi)
        cu = cu_ref[0, rows, :].astype(_F32)
        if sub == 0:
            prev_row = jnp.where(j > 0, prev_ref[0, HALO_ROWS - 1:HALO_ROWS, :].astype(_F32), 0.0)
        else:
            prev_row = cu_ref[0, lo - 1:lo, :].astype(_F32)
        if sub == n_sub - 1:
            next_row = jnp.where(j < pl.num_programs(1) - 1, next_ref[0, 0:1, :].astype(_F32), 0.0)
        else:
            next_row = cu_ref[0, hi:hi + 1, :].astype(_F32)
        before = jnp.where(row == 0, prev_row, pltpu.roll(cu, 1, 0))
        after = jnp.where(row == TAIL_SUB_ROWS - 1, next_row, pltpu.roll(cu, TAIL_SUB_ROWS - 1, 0))
        z = before * cw_ref[0:1, :] + cu * cw_ref[1:2, :] + after * cw_ref[2:3, :]
        gated = (gate_ref[0, rows, :].astype(_F32) * z).astype(_BF16)
        y = jnp.dot(gated, w_o_ref[...], preferred_element_type=_F32)
        _ffn_norm(x_ref[0, rows, :] + mod_ref[:, 2 * d:3 * d] * y, mod_ref, g_ref, out_ref, h_ref, rows)

    def stage_b(sub):
        _ffn_matmuls(mod_ref, w_in_ref, w_out_ref, out_ref, h_ref, act_ref, _tail_rows(sub),
                     finish=lambda x2: _rms_norm(x2, gf_ref[...]))

    _staggered(n_sub, stage_a, stage_b)


def _conv_out(x, gate, cu, mod, conv_w, w_o, g, w_in, w_out, final_g):
    b, s, d = x.shape
    f = FFN_HIDDEN
    tm = TAIL_ROWS
    per_tile = tm // HALO_ROWS
    n_halo = s // HALO_ROWS
    tile = pl.BlockSpec((1, tm, d), lambda i, j: (i, j, 0))
    return pl.pallas_call(
        functools.partial(_conv_out_kernel, tm=tm),
        grid=(b, s // tm),
        in_specs=[
            tile, tile, tile,
            pl.BlockSpec((1, HALO_ROWS, d), lambda i, j: (i, jnp.maximum(j * per_tile - 1, 0), 0)),
            pl.BlockSpec((1, HALO_ROWS, d), lambda i, j: (i, jnp.minimum((j + 1) * per_tile, n_halo - 1), 0)),
            pl.BlockSpec((None, 1, 6 * d), lambda i, j: (i, 0, 0)),
            pl.BlockSpec((CONV_WIDTH, d), lambda i, j: (0, 0)),
            _resident((d, d), lambda i, j: (0, 0)),
            pl.BlockSpec((1, d), lambda i, j: (0, 0)),
            _resident((d, 2 * f), lambda i, j: (0, 0)),
            _resident((f, d), lambda i, j: (0, 0)),
            pl.BlockSpec((1, d), lambda i, j: (0, 0)),
        ],
        out_specs=tile,
        out_shape=jax.ShapeDtypeStruct((b, s, d), _F32),
        scratch_shapes=[pltpu.VMEM((tm, f), _BF16), pltpu.VMEM((tm, d), _BF16)],
        compiler_params=_compiler_params(2),
        name="conv_out",
    )(x, gate, cu, cu, cu, mod, conv_w, w_o, g, w_in, w_out, final_g)


def kernel(x, c, attn_w_qkv, attn_w_o, conv_w_in, conv_w, conv_w_out, ada_w, ada_b,
           norm_mix_g, norm_ffn_g, ffn_w_in, ffn_w_out, final_g):
    b, s, d = x.shape
    assert d == D_MODEL and attn_w_qkv.shape[0] == 1 and conv_w_in.shape[0] == 1 and ada_w.shape[0] == 2
    mod, w_qkv = _modulation(c, ada_w, ada_b, attn_w_qkv)
    mod = mod.reshape(2, b, 1, 6 * d)

    qkv = _qkv(x, mod[0], norm_mix_g[0:1], w_qkv)
    o, w_o, w_in0, w_out0, w_ci, w_co, w_in1, w_out1 = _attention(
        qkv, s, [(attn_w_o, 0), (ffn_w_in, 0), (ffn_w_out, 0),
                 (conv_w_in, 0), (conv_w_out, 0), (ffn_w_in, 1), (ffn_w_out, 1)])
    x = _post_attention(x, o, mod[0], w_o, norm_ffn_g[0:1], w_in0, w_out0)

    gate, cu = _conv_in(x, mod[1], norm_mix_g[1:2], w_ci)
    return _conv_out(x, gate, cu, mod[1], conv_w[0], w_co, norm_ffn_g[1:2],
                     w_in1, w_out1, final_g.reshape(1, d))
```

```python
import functools

import numpy as np
import jax
import jax.numpy as jnp
from jax import lax
from jax.experimental import pallas as pl
from jax.experimental.pallas import tpu as pltpu

D_MODEL = 1024
HEAD_DIM = 64
N_HEADS = D_MODEL // HEAD_DIM
DILATED_GROUPS = ((128, 1), (512, 4), (2048, 16))
ROPE_THETA = 500000.0
ROPE_DIM = HEAD_DIM // 4
CONV_WIDTH = 3
FFN_HIDDEN = -(-8 * D_MODEL // (3 * 256)) * 256
NORM_EPS = 1e-6
NEG_INF = -1e30

LANES = 128
MXU_WIDTH = 256
HALF_WINDOW = 64
Q_BLOCK = 128
SCORE_SCALE = HEAD_DIM ** -0.5 * float(np.log2(np.e))
VMEM_LIMIT_BYTES = 56 * 1024 * 1024

assert all(w // (2 * d) == HALF_WINDOW for w, d in DILATED_GROUPS)

_F32 = jnp.float32
_BF16 = jnp.bfloat16


def _compiler_params(n_grid_dims):
    return pltpu.CompilerParams(
        dimension_semantics=("arbitrary",) * n_grid_dims,
        vmem_limit_bytes=VMEM_LIMIT_BYTES,
    )


def _resident(block_shape, index_map):
    return pl.BlockSpec(block_shape, index_map, pipeline_mode=pl.Buffered(1))


def _rms_norm(x, g):
    r = lax.rsqrt(jnp.mean(x * x, axis=-1, keepdims=True) + NORM_EPS)
    return (x * r) * g


def _modulated_norm(x, g, shift, scale):
    return _rms_norm(x, g) * (1.0 + scale) + shift


MODULATION_COLS = 1536


def _modulation_kernel(c_ref, w_ref, b_ref, wq_ref, o_ref, wq_bf_ref):
    c = c_ref[...]
    cond = (c * jax.nn.sigmoid(c)).astype(_BF16)
    o_ref[...] = jnp.dot(cond, w_ref[...].astype(_BF16), preferred_element_type=_F32) + b_ref[...]
    wq_bf_ref[...] = wq_ref[...].astype(_BF16)


def _modulation(c, ada_w, ada_b, w_qkv):
    depth, d, n = ada_w.shape
    b = c.shape[0]
    tn = MODULATION_COLS
    n_j = n // tn
    rows, cols = w_qkv.shape[1:]
    wq_rows = rows // (depth * n_j)
    assert wq_rows % CAST_ROWS == 0
    return pl.pallas_call(
        _modulation_kernel,
        grid=(depth, n_j),
        in_specs=[
            pl.BlockSpec((b, d), lambda i, j: (0, 0)),
            pl.BlockSpec((None, d, tn), lambda i, j: (i, 0, j)),
            pl.BlockSpec((None, 1, tn), lambda i, j: (i, 0, j)),
            pl.BlockSpec((None, wq_rows, cols), lambda i, j: (0, i * n_j + j, 0)),
        ],
        out_specs=[pl.BlockSpec((None, b, tn), lambda i, j: (i, 0, j)),
                   pl.BlockSpec((wq_rows, cols), lambda i, j: (i * n_j + j, 0))],
        out_shape=[jax.ShapeDtypeStruct((depth, b, n), _F32), jax.ShapeDtypeStruct((rows, cols), _BF16)],
        compiler_params=_compiler_params(2),
        name="modulation",
    )(c, ada_w, ada_b.reshape(depth, 1, n), w_qkv)


CAST_ROWS = 16


def _cast_specs(weights, n_inner, total_steps):
    in_specs, out_specs, out_shapes, args, pers = [], [], [], [], []
    for w, layer in weights:
        rows, cols = w.shape[1:]
        n_blocks = max(nb for nb in range(1, total_steps + 1)
                       if total_steps % nb == 0 and rows % (nb * CAST_ROWS) == 0)
        block_rows = rows // n_blocks
        block = lambda i, j, per=total_steps // n_blocks: (i * n_inner + j) // per
        in_specs.append(pl.BlockSpec((None, block_rows, cols),
                                     lambda i, j, layer=layer, block=block: (layer, block(i, j), 0)))
        out_specs.append(pl.BlockSpec((block_rows, cols), lambda i, j, block=block: (block(i, j), 0)))
        out_shapes.append(jax.ShapeDtypeStruct((rows, cols), _BF16))
        args.append(w)
        pers.append(total_steps // n_blocks)
    return in_specs, out_specs, out_shapes, args, tuple(pers)


CAST_GUARD_STEPS = 4


def _cast_blocks(in_refs, out_refs, pers, step, guarded):
    for w_ref, o_ref, per in zip(in_refs, out_refs, pers):
        if guarded and per >= CAST_GUARD_STEPS:
            @pl.when(step % per == 0)
            def _(w_ref=w_ref, o_ref=o_ref):
                o_ref[...] = w_ref[...].astype(_BF16)
        elif not guarded and per < CAST_GUARD_STEPS:
            o_ref[...] = w_ref[...].astype(_BF16)


def _rope_tables(seq, dilation):
    half = ROPE_DIM // 2
    length = seq // dilation
    inv = ROPE_THETA ** (-np.arange(half, dtype=np.float64) * (2.0 / ROPE_DIM))
    pos = (np.arange(length)[None, :] * dilation + np.arange(dilation)[:, None]).astype(np.float64)
    ang = pos[:, :, None] * inv[None, None, :]
    cos, sin = np.cos(ang), np.sin(ang)
    c = np.ones((dilation, length, HEAD_DIM))
    sa = np.zeros((dilation, length, HEAD_DIM))
    sb = np.zeros((dilation, length, HEAD_DIM))
    c[:, :, :half] = cos
    c[:, :, half:ROPE_DIM] = cos
    sa[:, :, :half] = -sin
    sb[:, :, half:ROPE_DIM] = sin
    return np.stack([np.tile(t, (1, 1, LANES // HEAD_DIM)) for t in (c, sa, sb)]).astype(np.float32)


QKV_ROWS = 512
QKV_SUB_ROWS = 256


def _staggered(n_groups, stage_a, stage_b):
    stage_a(0)
    for k in range(n_groups):
        if k + 1 < n_groups:
            stage_a(k + 1)
        stage_b(k)


def _qkv_kernel(x_ref, mod_ref, g_ref, w_ref, tab1_ref, tab4_ref, tab16_ref, o1_ref, o4_ref, o16_ref,
                lhs_ref, hs1_ref, hs4_ref):
    d, tm = D_MODEL, QKV_SUB_ROWS
    d4 = DILATED_GROUPS[1][1]
    n_slabs = d // LANES

    def stage_a(sub):
        x = x_ref[0, sub * tm:(sub + 1) * tm, :]
        h = _modulated_norm(x, g_ref[...], mod_ref[:, 0:d], mod_ref[:, d:2 * d])
        for c in range(n_slabs):
            hs1_ref[c] = h[:, c * LANES:(c + 1) * LANES]
        rows4 = tm // d4
        for c in range(n_slabs):
            hs4_ref[c] = jnp.concatenate([hs1_ref[c, pl.ds(r, rows4, stride=d4), :] for r in range(d4)], axis=0)
        h4 = jnp.concatenate([hs4_ref[c] for c in range(n_slabs)], axis=1)
        rows16 = rows4 // d4
        h16 = jnp.concatenate(
            [jnp.concatenate([hs4_ref[c, pl.ds((r16 % d4) * rows4 + r16 // d4, rows16, stride=d4), :]
                              for c in range(n_slabs)], axis=1)
             for r16 in range(d4 * d4)], axis=0)
        for g, hg in enumerate((h, h4, h16)):
            lhs_ref[sub, g] = hg.astype(_BF16)

    def stage_b(sub):
        for g, (tab_ref, o_ref) in enumerate(((tab1_ref, o1_ref), (tab4_ref, o4_ref), (tab16_ref, o16_ref))):
            dilation = DILATED_GROUPS[g][1]
            rows = tm // dilation
            out_rows = slice(sub * rows, (sub + 1) * rows)
            hb = lhs_ref[sub, g]
            for part in range(3):
                for j in range(d // MXU_WIDTH):
                    col = part * d + j * MXU_WIDTH
                    res = jnp.dot(hb, w_ref[:, g * 3 * d + col:g * 3 * d + col + MXU_WIDTH],
                                  preferred_element_type=_F32)
                    if part < 2:
                        halves = []
                        for s in range(MXU_WIDTH // LANES):
                            xs = res[:, s * LANES:(s + 1) * LANES]
                            tabs = [jnp.concatenate([tab_ref[t, r, out_rows, :] for r in range(dilation)], axis=0)
                                    for t in range(3)]
                            halves.append(xs * tabs[0]
                                          + pltpu.roll(xs, LANES - ROPE_DIM // 2, 1) * tabs[1]
                                          + pltpu.roll(xs, ROPE_DIM // 2, 1) * tabs[2])
                        res = jnp.concatenate(halves, axis=1)
                        if part == 0:
                            res = res * SCORE_SCALE
                    res = res.astype(_BF16)
                    for r in range(dilation):
                        o_ref[0, r, out_rows, col:col + MXU_WIDTH] = res[r * rows:(r + 1) * rows]

    _staggered(QKV_ROWS // QKV_SUB_ROWS, stage_a, stage_b)


def _qkv(x, mod, g, w):
    b, s, d = x.shape
    tm = QKV_ROWS
    dils = [dil for _, dil in DILATED_GROUPS]
    assert dils == [1, 4, 16]
    tabs = [jnp.asarray(_rope_tables(s, dil)) for dil in dils]
    return pl.pallas_call(
        _qkv_kernel,
        grid=(b, s // tm),
        in_specs=[
            pl.BlockSpec((1, tm, d), lambda i, j: (i, j, 0)),
            pl.BlockSpec((None, 1, 6 * d), lambda i, j: (i, 0, 0)),
            pl.BlockSpec((1, d), lambda i, j: (0, 0)),
            _resident(w.shape, lambda i, j: (0, 0)),
        ] + [pl.BlockSpec((3, dil, tm // dil, LANES), lambda i, j: (0, 0, j, 0)) for dil in dils],
        out_specs=[pl.BlockSpec((1, dil, tm // dil, 3 * d), lambda i, j: (i, 0, j, 0)) for dil in dils],
        out_shape=[jax.ShapeDtypeStruct((b, dil, s // dil, 3 * d), _BF16) for dil in dils],
        scratch_shapes=[pltpu.VMEM((QKV_ROWS // QKV_SUB_ROWS, len(dils), QKV_SUB_ROWS, d), _BF16)]
        + [pltpu.VMEM((d // LANES, QKV_SUB_ROWS, LANES), _F32)] * 2,
        compiler_params=_compiler_params(2),
        name="qkv",
    )(x, mod, g, w, *tabs)


def _band_bias(n_keys):
    i = np.arange(Q_BLOCK)[:, None]
    j = np.arange(n_keys)[None, :]
    one = np.stack([np.where(np.abs(off + i - j) <= HALF_WINDOW, 0.0, NEG_INF)
                    for off in (0, HALF_WINDOW, 2 * HALF_WINDOW)])
    return np.concatenate([one, one], axis=1).astype(np.float32)


def _pair_bias():
    band = _band_bias(Q_BLOCK)[0]
    masked = np.full_like(band, NEG_INF)
    return np.stack([np.concatenate([band, masked], axis=1), np.concatenate([masked, band], axis=1)])


ATTN_UNROLL = 16
ATTN_PAIRS = 2


def _attention_kernel(q0, k0, v0, q1, k1, v1, q2, k2, v2, bias_w_ref, bias_p_ref, *refs, seq, cast_pers):
    n_cast = len(cast_pers)
    cast_in, o_ref = refs[:n_cast], refs[n_cast]
    cast_out, (s4_ref, s1_ref) = refs[n_cast + 1:2 * n_cast + 1], refs[2 * n_cast + 1:]
    step = pl.program_id(0) * pl.num_programs(1) + pl.program_id(1)
    _cast_blocks(cast_in, cast_out, cast_pers, step, True)
    lane = lax.broadcasted_iota(jnp.int32, (Q_BLOCK, LANES), 1)
    head_a = lane < HEAD_DIM
    for pair in range(ATTN_PAIRS):
        if pair == 1:
            _cast_blocks(cast_in, cast_out, cast_pers, step, False)
        _attention_pair(q0, k0, v0, q1, k1, v1, q2, k2, v2, bias_w_ref, bias_p_ref, o_ref,
                        s4_ref.at[pair], s1_ref.at[pair], pl.ds(pair * LANES, LANES), head_a, seq)


def _attention_pair(q0, k0, v0, q1, k1, v1, q2, k2, v2, bias_w_ref, bias_p_ref, o_ref,
                    s4_ref, s1_ref, lanes, head_a, seq):
    d4 = DILATED_GROUPS[1][1]
    len4 = seq // d4

    def block_stats(q, k, v, bias):
        n_keys = k.shape[0]
        zero = jnp.zeros_like(q)
        q2 = jnp.concatenate([jnp.where(head_a, q, zero), jnp.where(head_a, zero, q)], axis=0)
        s = lax.dot_general(q2, k, (((1,), (1,)), ((), ())), preferred_element_type=_F32) + bias
        m = jnp.max(s, axis=-1, keepdims=True)
        p = jnp.exp2(s - m).astype(_BF16)
        v_ext = jnp.concatenate([v, jnp.ones((n_keys, LANES), _BF16)], axis=1)
        pv = jnp.dot(p, v_ext, preferred_element_type=_F32)
        m_pair = jnp.where(head_a, jnp.broadcast_to(m[:Q_BLOCK], (Q_BLOCK, LANES)),
                           jnp.broadcast_to(m[Q_BLOCK:], (Q_BLOCK, LANES)))
        l_pair = jnp.where(head_a, pv[:Q_BLOCK, LANES:], pv[Q_BLOCK:, LANES:])
        acc_pair = jnp.where(head_a, pv[:Q_BLOCK, :LANES], pv[Q_BLOCK:, :LANES])
        return m_pair, l_pair, acc_pair

    def merge(old, new):
        m = jnp.maximum(old[0], new[0])
        a_old = jnp.exp2(old[0] - m)
        a_new = jnp.exp2(new[0] - m)
        return m, a_old * old[1] + a_new * new[1], a_old * old[2] + a_new * new[2]

    def load_block(q_ref, k_ref, v_ref, r, j, length):
        n_keys = min(length, Q_BLOCK + 2 * HALF_WINDOW)
        m0 = pl.multiple_of(j * Q_BLOCK, Q_BLOCK)
        k_start = pl.multiple_of(jnp.clip(m0 - HALF_WINDOW, 0, length - n_keys), HALF_WINDOW)
        bias = bias_w_ref[(m0 - k_start) // HALF_WINDOW]
        return (q_ref[0, r, pl.ds(m0, Q_BLOCK), lanes], k_ref[0, r, pl.ds(k_start, n_keys), lanes],
                v_ref[0, r, pl.ds(k_start, n_keys), lanes], bias)

    def load_pair_block(q_ref, k_ref, v_ref, r):
        first = pl.multiple_of((r // 2) * 2, 2)
        keys = lambda ref: ref[0, pl.ds(first, 2), :, lanes].reshape(2 * Q_BLOCK, LANES)
        return q_ref[0, r, :, lanes], keys(k_ref), keys(v_ref), bias_p_ref[r % 2]

    def group16(idx, carry):
        a, c = idx // d4, idx % d4
        state = block_stats(*load_pair_block(q2, k2, v2, idx))
        rows = pl.ds(c * len4 + a, Q_BLOCK, stride=d4)
        for t in range(3):
            s4_ref[t, rows, :] = state[t]
        return carry

    def group4(idx, carry):
        c, j = idx // (len4 // Q_BLOCK), idx % (len4 // Q_BLOCK)
        new = block_stats(*load_block(q1, k1, v1, c, j, len4))
        src = pl.ds(pl.multiple_of(c * len4 + j * Q_BLOCK, Q_BLOCK), Q_BLOCK)
        state = merge([s4_ref[t, src, :] for t in range(3)], new)
        rows = pl.ds(j * Q_BLOCK * d4 + c, Q_BLOCK, stride=d4)
        for t in range(3):
            s1_ref[t, rows, :] = state[t]
        return carry

    def group1(j, carry):
        new = block_stats(*load_block(q0, k0, v0, 0, j, seq))
        rows = pl.ds(pl.multiple_of(j * Q_BLOCK, Q_BLOCK), Q_BLOCK)
        _, l, acc = merge([s1_ref[t, rows, :] for t in range(3)], new)
        o_ref[0, rows, lanes] = (acc / l).astype(o_ref.dtype)
        return carry

    n_blocks = seq // Q_BLOCK
    lax.fori_loop(0, n_blocks, group16, 0, unroll=ATTN_UNROLL)
    lax.fori_loop(0, n_blocks, group4, 0, unroll=ATTN_UNROLL)
    lax.fori_loop(0, n_blocks, group1, 0, unroll=ATTN_UNROLL)


def _attention(qkv_groups, seq, cast_weights):
    b = qkv_groups[0].shape[0]
    d = D_MODEL
    width = ATTN_PAIRS * LANES
    n_steps = d // width
    in_specs, args = [], []
    for qkv, (_, dil) in zip(qkv_groups, DILATED_GROUPS):
        for part in range(3):
            in_specs.append(pl.BlockSpec((1, dil, seq // dil, width),
                                         lambda i, p, part=part: (i, 0, 0, part * n_steps + p)))
            args.append(qkv)
    wide = Q_BLOCK + 2 * HALF_WINDOW
    assert seq // DILATED_GROUPS[2][1] == Q_BLOCK and seq // DILATED_GROUPS[1][1] >= wide
    in_specs += [pl.BlockSpec((3, 2 * Q_BLOCK, wide), lambda i, p: (0, 0, 0)),
                 pl.BlockSpec((2, 2 * Q_BLOCK, wide), lambda i, p: (0, 0, 0))]
    args += [jnp.asarray(_band_bias(wide)), jnp.asarray(_pair_bias())]
    c_in, c_out, c_shapes, c_args, c_pers = _cast_specs(cast_weights, n_steps, b * n_steps)
    return pl.pallas_call(
        functools.partial(_attention_kernel, seq=seq, cast_pers=c_pers),
        grid=(b, n_steps),
        in_specs=in_specs + c_in,
        out_specs=[pl.BlockSpec((1, seq, width), lambda i, p: (i, 0, p))] + c_out,
        out_shape=[jax.ShapeDtypeStruct((b, seq, d), _BF16)] + c_shapes,
        scratch_shapes=[pltpu.VMEM((ATTN_PAIRS, 3, seq, LANES), _F32),
                        pltpu.VMEM((ATTN_PAIRS, 3, seq, LANES), _F32)],
        compiler_params=_compiler_params(2),
        name="dilated_attention",
    )(*args, *c_args)


FFN_CHUNK = MXU_WIDTH


TAIL_ROWS = 1024
TAIL_SUB_ROWS = 256


def _ffn_norm(x1, mod_ref, g_ref, out_ref, h_ref, rows):
    d = D_MODEL
    out_ref[0, rows, :] = x1
    h_ref[rows, :] = _modulated_norm(x1, g_ref[...], mod_ref[:, 3 * d:4 * d], mod_ref[:, 4 * d:5 * d]).astype(_BF16)


def _ffn_matmuls(mod_ref, w_in_ref, w_out_ref, out_ref, h_ref, act_ref, rows, finish=None):
    d, f = D_MODEL, FFN_HIDDEN
    h = h_ref[rows, :]
    for c in range(f // FFN_CHUNK):
        lo = c * FFN_CHUNK
        gate = jnp.dot(h, w_in_ref[:, lo:lo + FFN_CHUNK], preferred_element_type=_F32)
        up = jnp.dot(h, w_in_ref[:, f + lo:f + lo + FFN_CHUNK], preferred_element_type=_F32)
        half = 0.5 * gate
        act_ref[rows, lo:lo + FFN_CHUNK] = ((half + half * jnp.tanh(half)) * up).astype(_BF16)
    y = jnp.dot(act_ref[rows, :], w_out_ref[...], preferred_element_type=_F32)
    res = out_ref[0, rows, :] + mod_ref[:, 5 * d:6 * d] * y
    out_ref[0, rows, :] = res if finish is None else finish(res)


def _tail_rows(sub):
    return slice(sub * TAIL_SUB_ROWS, (sub + 1) * TAIL_SUB_ROWS)


def _post_attention_kernel(x_ref, o_ref, mod_ref, w_o_ref, g_ref, w_in_ref, w_out_ref, out_ref, act_ref, h_ref):
    d = D_MODEL

    def stage_a(sub):
        rows = _tail_rows(sub)
        y = jnp.dot(o_ref[0, rows, :], w_o_ref[...], preferred_element_type=_F32)
        _ffn_norm(x_ref[0, rows, :] + mod_ref[:, 2 * d:3 * d] * y, mod_ref, g_ref, out_ref, h_ref, rows)

    def stage_b(sub):
        _ffn_matmuls(mod_ref, w_in_ref, w_out_ref, out_ref, h_ref, act_ref, _tail_rows(sub))

    _staggered(TAIL_ROWS // TAIL_SUB_ROWS, stage_a, stage_b)


def _post_attention(x, o, mod, w_o, g, w_in, w_out):
    b, s, d = x.shape
    f = FFN_HIDDEN
    tm = TAIL_ROWS
    return pl.pallas_call(
        _post_attention_kernel,
        grid=(b, s // tm),
        in_specs=[
            pl.BlockSpec((1, tm, d), lambda i, j: (i, j, 0)),
            pl.BlockSpec((1, tm, d), lambda i, j: (i, j, 0)),
            pl.BlockSpec((None, 1, 6 * d), lambda i, j: (i, 0, 0)),
            _resident((d, d), lambda i, j: (0, 0)),
            pl.BlockSpec((1, d), lambda i, j: (0, 0)),
            _resident((d, 2 * f), lambda i, j: (0, 0)),
            _resident((f, d), lambda i, j: (0, 0)),
        ],
        out_specs=pl.BlockSpec((1, tm, d), lambda i, j: (i, j, 0)),
        out_shape=jax.ShapeDtypeStruct((b, s, d), _F32),
        scratch_shapes=[pltpu.VMEM((tm, f), _BF16), pltpu.VMEM((tm, d), _BF16)],
        compiler_params=_compiler_params(2),
        name="post_attention",
    )(x, o, mod, w_o, g, w_in, w_out)


CONV_IN_ROWS = 1024
CONV_IN_SUB_ROWS = 256


def _conv_in_kernel(x_ref, mod_ref, g_ref, w_ref, gate_ref, cu_ref, h_ref):
    d = D_MODEL

    def group_rows(sub):
        return slice(sub * CONV_IN_SUB_ROWS, (sub + 1) * CONV_IN_SUB_ROWS)

    def stage_a(sub):
        rows = group_rows(sub)
        h_ref[rows, :] = _modulated_norm(
            x_ref[0, rows, :], g_ref[...], mod_ref[:, 0:d], mod_ref[:, d:2 * d]).astype(_BF16)

    def stage_b(sub):
        rows = group_rows(sub)
        h = h_ref[rows, :]
        for j in range(d // MXU_WIDTH):
            lo = j * MXU_WIDTH
            gate_ref[0, rows, lo:lo + MXU_WIDTH] = jnp.dot(
                h, w_ref[:, lo:lo + MXU_WIDTH], preferred_element_type=_F32).astype(_BF16)
            c_gate = jnp.dot(h, w_ref[:, d + lo:d + lo + MXU_WIDTH], preferred_element_type=_F32)
            u = jnp.dot(h, w_ref[:, 2 * d + lo:2 * d + lo + MXU_WIDTH], preferred_element_type=_F32)
            cu_ref[0, rows, lo:lo + MXU_WIDTH] = (c_gate * u).astype(_BF16)

    _staggered(CONV_IN_ROWS // CONV_IN_SUB_ROWS, stage_a, stage_b)


def _conv_in(x, mod, g, w):
    b, s, d = x.shape
    tm = CONV_IN_ROWS
    tile = pl.BlockSpec((1, tm, d), lambda i, j: (i, j, 0))
    return pl.pallas_call(
        _conv_in_kernel,
        grid=(b, s // tm),
        in_specs=[
            tile,
            pl.BlockSpec((None, 1, 6 * d), lambda i, j: (i, 0, 0)),
            pl.BlockSpec((1, d), lambda i, j: (0, 0)),
            _resident((d, 3 * d), lambda i, j: (0, 0)),
        ],
        out_specs=[tile, tile],
        out_shape=[jax.ShapeDtypeStruct((b, s, d), _BF16)] * 2,
        scratch_shapes=[pltpu.VMEM((tm, d), _BF16)],
        compiler_params=_compiler_params(2),
        name="conv_in",
    )(x, mod, g, w)


HALO_ROWS = 16


def _conv_out_kernel(x_ref, gate_ref, cu_ref, prev_ref, next_ref, mod_ref, cw_ref, w_o_ref, g_ref,
                     w_in_ref, w_out_ref, gf_ref, out_ref, act_ref, h_ref, *, tm):
    d = D_MODEL
    j = pl.program_id(1)
    n_sub = tm // TAIL_SUB_ROWS
    row = lax.broadcasted_iota(jnp.int32, (TAIL_SUB_ROWS, d), 0)

    def stage_a(sub):
        lo, hi = sub * TAIL_SUB_ROWS, (sub + 1) * TAIL_SUB_ROWS
        rows = slice(lo, hi)
        cu = cu_ref[0, rows, :].astype(_F32)
        if sub == 0:
            prev_row = jnp.where(j > 0, prev_ref[0, HALO_ROWS - 1:HALO_ROWS, :].astype(_F32), 0.0)
        else:
            prev_row = cu_ref[0, lo - 1:lo, :].astype(_F32)
        if sub == n_sub - 1:
            next_row = jnp.where(j < pl.num_programs(1) - 1, next_ref[0, 0:1, :].astype(_F32), 0.0)
        else:
            next_row = cu_ref[0, hi:hi + 1, :].astype(_F32)
        before = jnp.where(row == 0, prev_row, pltpu.roll(cu, 1, 0))
        after = jnp.where(row == TAIL_SUB_ROWS - 1, next_row, pltpu.roll(cu, TAIL_SUB_ROWS - 1, 0))
        z = before * cw_ref[0:1, :] + cu * cw_ref[1:2, :] + after * cw_ref[2:3, :]
        gated = (gate_ref[0, rows, :].astype(_F32) * z).astype(_BF16)
        y = jnp.dot(gated, w_o_ref[...], preferred_element_type=_F32)
        _ffn_norm(x_ref[0, rows, :] + mod_ref[:, 2 * d:3 * d] * y, mod_ref, g_ref, out_ref, h_ref, rows)

    def stage_b(sub):
        _ffn_matmuls(mod_ref, w_in_ref, w_out_ref, out_ref, h_ref, act_ref, _tail_rows(sub),
                     finish=lambda x2: _rms_norm(x2, gf_ref[...]))

    _staggered(n_sub, stage_a, stage_b)


def _conv_out(x, gate, cu, mod, conv_w, w_o, g, w_in, w_out, final_g):
    b, s, d = x.shape
    f = FFN_HIDDEN
    tm = TAIL_ROWS
    per_tile = tm // HALO_ROWS
    n_halo = s // HALO_ROWS
    tile = pl.BlockSpec((1, tm, d), lambda i, j: (i, j, 0))
    return pl.pallas_call(
        functools.partial(_conv_out_kernel, tm=tm),
        grid=(b, s // tm),
        in_specs=[
            tile, tile, tile,
            pl.BlockSpec((1, HALO_ROWS, d), lambda i, j: (i, jnp.maximum(j * per_tile - 1, 0), 0)),
            pl.BlockSpec((1, HALO_ROWS, d), lambda i, j: (i, jnp.minimum((j + 1) * per_tile, n_halo - 1), 0)),
            pl.BlockSpec((None, 1, 6 * d), lambda i, j: (i, 0, 0)),
            pl.BlockSpec((CONV_WIDTH, d), lambda i, j: (0, 0)),
            _resident((d, d), lambda i, j: (0, 0)),
            pl.BlockSpec((1, d), lambda i, j: (0, 0)),
            _resident((d, 2 * f), lambda i, j: (0, 0)),
            _resident((f, d), lambda i, j: (0, 0)),
            pl.BlockSpec((1, d), lambda i, j: (0, 0)),
        ],
        out_specs=tile,
        out_shape=jax.ShapeDtypeStruct((b, s, d), _F32),
        scratch_shapes=[pltpu.VMEM((tm, f), _BF16), pltpu.VMEM((tm, d), _BF16)],
        compiler_params=_compiler_params(2),
        name="conv_out",
    )(x, gate, cu, cu, cu, mod, conv_w, w_o, g, w_in, w_out, final_g)


def kernel(x, c, attn_w_qkv, attn_w_o, conv_w_in, conv_w, conv_w_out, ada_w, ada_b,
           norm_mix_g, norm_ffn_g, ffn_w_in, ffn_w_out, final_g):
    b, s, d = x.shape
    assert d == D_MODEL and attn_w_qkv.shape[0] == 1 and conv_w_in.shape[0] == 1 and ada_w.shape[0] == 2
    mod, w_qkv = _modulation(c, ada_w, ada_b, attn_w_qkv)
    mod = mod.reshape(2, b, 1, 6 * d)

    qkv = _qkv(x, mod[0], norm_mix_g[0:1], w_qkv)
    o, w_o, w_in0, w_out0, w_ci, w_co, w_in1, w_out1 = _attention(
        qkv, s, [(attn_w_o, 0), (ffn_w_in, 0), (ffn_w_out, 0),
                 (conv_w_in, 0), (conv_w_out, 0), (ffn_w_in, 1), (ffn_w_out, 1)])
    x = _post_attention(x, o, mod[0], w_o, norm_ffn_g[0:1], w_in0, w_out0)

    gate, cu = _conv_in(x, mod[1], norm_mix_g[1:2], w_ci)
    return _conv_out(x, gate, cu, mod[1], conv_w[0], w_co, norm_ffn_g[1:2],
                     w_in1, w_out1, final_g.reshape(1, d))
```
